```python
import math
import jax, jax.numpy as jnp
from jax import lax
import numpy as np

D_MODEL = 1024
BATCH = 16
SEQ = 2048
DEPTH = 2

D_MIX = D_MODEL
D_ATTN = D_MIX // 2
D_CONV = D_MIX - D_ATTN
HEAD_DIM = 64
N_HEADS = D_ATTN // HEAD_DIM
CONV_GROUP = 64
N_CONV_GROUPS = D_CONV // CONV_GROUP
DILATED_CONFIGS = ((128, 1), (512, 4), (2048, 16))
BAND_BLOCK = 128
CONV_WIDTH = 3
FFN_CONV_WIDTH = 3
D_FF = 2816
EPS = 1e-6

kernel_name = "hybrid_dilated_attn_shortconv_convffn"


def rmsnorm(x, g):
    xf = x.astype(jnp.float32)
    y = xf * lax.rsqrt(jnp.mean(xf * xf, axis=-1, keepdims=True) + EPS)
    return (y * g.astype(jnp.float32)).astype(x.dtype)


def group_rmsnorm(x, g, n_groups):
    shp = x.shape
    xg = x.reshape(*shp[:-1], n_groups, shp[-1] // n_groups)
    return rmsnorm(xg, g.reshape(n_groups, -1)).reshape(shp)


def causal_dwconv(u, w):
    K = w.shape[0]
    S = u.shape[1]
    up = jnp.pad(u, ((0, 0), (K - 1, 0), (0, 0)))
    return sum(up[:, k:k + S] * w[k].astype(u.dtype) for k in range(K))


def alibi_slopes(n):
    return 2.0 ** (-8.0 * jnp.arange(1, n + 1, dtype=jnp.float32) / n)


def dilated_branch(q, k, v, slopes, window, dilation):
    B, H, S, hd = q.shape
    span = window // dilation
    L = S // dilation
    nb = -(-L // BAND_BLOCK)
    Lp = nb * BAND_BLOCK

    def to_blocks(t):
        t = t.reshape(B, H, L, dilation, hd).transpose(0, 1, 3, 2, 4)
        t = jnp.pad(t, ((0, 0), (0, 0), (0, 0), (0, Lp - L), (0, 0)))
        return t.reshape(B, H, dilation, nb, BAND_BLOCK, hd)

    def with_prev(t):
        prev = jnp.pad(t, ((0, 0), (0, 0), (0, 0), (1, 0), (0, 0), (0, 0)))[:, :, :, :-1]
        return jnp.concatenate([prev, t], axis=-2)

    qb, kb, vb = to_blocks(q), to_blocks(k), to_blocks(v)
    kk, vv = with_prev(kb), with_prev(vb)
    s = jnp.einsum('bhrnqd,bhrnkd->bhrnqk', qb, kk)
    i = jnp.arange(BAND_BLOCK)[:, None]
    j = jnp.arange(2 * BAND_BLOCK)[None, :]
    dist = BAND_BLOCK + i - j
    blk = jnp.arange(nb)[:, None, None]
    valid = (dist >= 0) & (dist <= span) & ((blk > 0) | (j >= BAND_BLOCK))
    bias = -(slopes * dilation).reshape(1, H, 1, 1, 1, 1) * dist.astype(jnp.float32)
    s = jnp.where(valid, s + bias, -jnp.inf)
    m = jnp.max(s, axis=-1)
    p = jnp.exp(s - m[..., None])
    l = jnp.sum(p, axis=-1)
    o = jnp.einsum('bhrnqk,bhrnkd->bhrnqd', p, vv)

    def from_blocks(t):
        rest = t.shape[5:]
        t = t.reshape(B, H, dilation, Lp, *rest)[:, :, :, :L]
        t = jnp.moveaxis(t, 2, 3)
        return t.reshape(B, H, S, *rest)

    return from_blocks(o), from_blocks(m), from_blocks(l)


def dilated_attention(q, k, v):
    slopes = alibi_slopes(q.shape[1])
    outs = [dilated_branch(q, k, v, slopes, w, d) for (w, d) in DILATED_CONFIGS]
    m_max = jnp.max(jnp.stack([m for (_, m, _) in outs]), axis=0)
    num = sum(o * jnp.exp(m - m_max)[..., None] for (o, m, _) in outs)
    den = sum(l * jnp.exp(m - m_max) for (_, m, l) in outs)
    return num / den[..., None]


def hybrid_mixer(h, w_in, mix_conv_w, attn_out_g, conv_out_g, w_out):
    B, S, _ = h.shape
    proj = h @ w_in
    q, k, v, gate_b, gate_c, u = jnp.split(
        proj, [D_ATTN, 2 * D_ATTN, 3 * D_ATTN, 3 * D_ATTN + D_CONV, 3 * D_ATTN + 2 * D_CONV], axis=-1)

    def heads(t):
        return t.reshape(B, S, N_HEADS, HEAD_DIM).transpose(0, 2, 1, 3).astype(jnp.float32)

    attn = dilated_attention(heads(q) * (HEAD_DIM ** -0.5), heads(k), heads(v))
    attn = attn.transpose(0, 2, 1, 3).reshape(B, S, D_ATTN).astype(h.dtype)
    attn = group_rmsnorm(attn, attn_out_g, N_HEADS)

    y = gate_b * causal_dwconv(gate_c * u, mix_conv_w)
    y = group_rmsnorm(y, conv_out_g, N_CONV_GROUPS)

    return jnp.concatenate([attn, y], axis=-1) @ w_out


def conv_glu_ffn(h, ffn_up, ffn_conv_w, ffn_down):
    up = causal_dwconv(h @ ffn_up, ffn_conv_w)
    gate, val = jnp.split(up, 2, axis=-1)
    return (jax.nn.silu(gate) * val) @ ffn_down


def setup_inputs(seed: int = 0) -> dict:
    key = jax.random.key(seed)
    ks = jax.random.split(key, 12)
    f32 = jnp.float32
    n = jax.random.normal
    d_in = 3 * D_ATTN + 3 * D_CONV
    return {
        "x": n(ks[0], (BATCH, SEQ, D_MODEL), f32),
        "norm1_g": 1.0 + 0.02 * n(ks[1], (DEPTH, D_MODEL), f32),
        "w_in": n(ks[2], (DEPTH, D_MODEL, d_in), f32) * D_MODEL ** -0.5,
        "mix_conv_w": n(ks[3], (DEPTH, CONV_WIDTH, D_CONV), f32) * CONV_WIDTH ** -0.5,
        "attn_out_g": 1.0 + 0.02 * n(ks[4], (DEPTH, D_ATTN), f32),
        "conv_out_g": 1.0 + 0.02 * n(ks[5], (DEPTH, D_CONV), f32),
        "w_out": n(ks[6], (DEPTH, D_MIX, D_MODEL), f32) * D_MIX ** -0.5,
        "norm2_g": 1.0 + 0.02 * n(ks[7], (DEPTH, D_MODEL), f32),
        "ffn_up": n(ks[8], (DEPTH, D_MODEL, 2 * D_FF), f32) * D_MODEL ** -0.5,
        "ffn_conv_w": n(ks[9], (DEPTH, FFN_CONV_WIDTH, 2 * D_FF), f32) * FFN_CONV_WIDTH ** -0.5,
        "ffn_down": n(ks[10], (DEPTH, D_FF, D_MODEL), f32) * D_FF ** -0.5,
        "final_norm_g": 1.0 + 0.02 * n(ks[11], (D_MODEL,), f32),
    }


def reference(x, norm1_g, w_in, mix_conv_w, attn_out_g, conv_out_g, w_out,
              norm2_g, ffn_up, ffn_conv_w, ffn_down, final_norm_g):
    for layer in range(DEPTH):
        h = rmsnorm(x, norm1_g[layer])
        x = x + hybrid_mixer(h, w_in[layer], mix_conv_w[layer], attn_out_g[layer],
                             conv_out_g[layer], w_out[layer])
        h = rmsnorm(x, norm2_g[layer])
        x = x + conv_glu_ffn(h, ffn_up[layer], ffn_conv_w[layer], ffn_down[layer])
    return rmsnorm(x, final_norm_g)
```

```python
import functools
import math

import jax
import jax.numpy as jnp
from jax import lax
from jax.experimental import pallas as pl
from jax.experimental.pallas import tpu as pltpu

D_MODEL = 1024
D_ATTN = 512
D_CONV = 512
HEAD_DIM = 64
N_HEADS = D_ATTN // HEAD_DIM
DILATIONS = (1, 4, 16)
BAND = 128
D_FF = 2816
EPS = 1e-6

LANES = 128
N_SLAB = 3 * D_ATTN // LANES
N_PAIR = N_HEADS // 2
FF_CHUNK = 256
N_FF_CHUNK = D_FF // FF_CHUNK
TS = 256
LOG2E = 1.0 / math.log(2.0)
Q_SCALE = HEAD_DIM ** -0.5 * LOG2E
NEG = -1e30

f32 = jnp.float32
bf16 = jnp.bfloat16


def _rms_scale(x):
    return lax.rsqrt(jnp.mean(x * x, axis=-1, keepdims=True) + EPS)


def _shift_rows(x, prev8, k):
    rolled = pltpu.roll(x, k, axis=0)
    row = lax.broadcasted_iota(jnp.int32, prev8.shape, 0)
    head = jnp.where(row < k, pltpu.roll(prev8, k, axis=0), rolled[:8])
    return jnp.concatenate([head, rolled[8:]], axis=0)


def _mixer_in_kernel(x_ref, g_ref, w_ref, cw_ref, cg_ref, gmat_ref,
                     o1_ref, o4_ref, o16_ref, y_ref, nat_s, d4_s, carry_s):
    si = pl.program_id(1)
    x = x_ref[0]
    h = (x * _rms_scale(x) * g_ref[...]).astype(bf16)

    for j in range(N_SLAB):
        pj = jnp.dot(h, w_ref[:, j * LANES:(j + 1) * LANES], preferred_element_type=f32)
        if j < N_PAIR:
            pj = pj * Q_SCALE
        nat_s[j] = pj
        o1_ref[0, j] = pj.astype(bf16)
    for j in range(N_SLAB):
        for r in range(4):
            c4 = nat_s[j, pl.ds(r, TS // 4, stride=4), :]
            d4_s[j, r] = c4
            o4_ref[0, j, r] = c4.astype(bf16)
    for j in range(N_SLAB):
        for r in range(16):
            c16 = d4_s[j, r % 4, pl.ds(r // 4, TS // 16, stride=4), :]
            o16_ref[0, j, r] = c16.astype(bf16)

    c0 = 3 * D_ATTN
    gate_b = jnp.dot(h, w_ref[:, c0:c0 + D_CONV], preferred_element_type=f32)
    gate_c = jnp.dot(h, w_ref[:, c0 + D_CONV:c0 + 2 * D_CONV], preferred_element_type=f32)
    u = jnp.dot(h, w_ref[:, c0 + 2 * D_CONV:c0 + 3 * D_CONV], preferred_element_type=f32)
    cu = gate_c * u

    @pl.when(si == 0)
    def _():
        carry_s[...] = jnp.zeros_like(carry_s)

    prev = carry_s[...]
    carry_s[...] = cu[TS - 8:]
    cw = cw_ref[...]
    y = gate_b * (cw[0:1] * _shift_rows(cu, prev, 2) + cw[1:2] * _shift_rows(cu, prev, 1)
                  + cw[2:3] * cu)
    ss = jnp.dot((y * y).astype(bf16), gmat_ref[...], preferred_element_type=f32)
    y_ref[0] = (y * lax.rsqrt(ss * (1.0 / HEAD_DIM) + EPS) * cg_ref[...]).astype(bf16)


def _mixer_in(x, g, w_in, conv_w, conv_g, gmat):
    B, S, D = x.shape
    ns = S // TS
    const = dict(pipeline_mode=pl.Buffered(1))
    return pl.pallas_call(
        _mixer_in_kernel,
        grid=(B, ns),
        in_specs=[
            pl.BlockSpec((1, TS, D), lambda b, s: (b, s, 0)),
            pl.BlockSpec((1, D), lambda b, s: (0, 0), **const),
            pl.BlockSpec(w_in.shape, lambda b, s: (0, 0), **const),
            pl.BlockSpec(conv_w.shape, lambda b, s: (0, 0), **const),
            pl.BlockSpec((1, D_CONV), lambda b, s: (0, 0), **const),
            pl.BlockSpec(gmat.shape, lambda b, s: (0, 0), **const),
        ],
        out_specs=[
            pl.BlockSpec((1, N_SLAB, TS, LANES), lambda b, s: (b, 0, s, 0)),
            pl.BlockSpec((1, N_SLAB, 4, TS // 4, LANES), lambda b, s: (b, 0, 0, s, 0)),
            pl.BlockSpec((1, N_SLAB, 16, TS // 16, LANES), lambda b, s: (b, 0, 0, s, 0)),
            pl.BlockSpec((1, TS, D_CONV), lambda b, s: (b, s, 0)),
        ],
        out_shape=[
            jax.ShapeDtypeStruct((B, N_SLAB, S, LANES), bf16),
            jax.ShapeDtypeStruct((B, N_SLAB, 4, S // 4, LANES), bf16),
            jax.ShapeDtypeStruct((B, N_SLAB, 16, S // 16, LANES), bf16),
            jax.ShapeDtypeStruct((B, S, D_CONV), bf16),
        ],
        scratch_shapes=[
            pltpu.VMEM((N_SLAB, TS, LANES), f32),
            pltpu.VMEM((N_SLAB, 4, TS // 4, LANES), f32),
            pltpu.VMEM((8, D_CONV), f32),
        ],
        compiler_params=pltpu.CompilerParams(
            dimension_semantics=("arbitrary", "arbitrary"),
            vmem_limit_bytes=40 * 1024 * 1024),
        name="mixer_in",
    )(x, g, w_in, conv_w, conv_g, gmat)


def _attn_tile(q, kcat, vcat, bias_ref, bi, has_prev, head_masks, lo):
    outs, lses = [], []
    for hh, mk in enumerate(head_masks):
        s = lax.dot_general(q * mk, kcat, (((1,), (1,)), ((), ())), preferred_element_type=f32)
        if has_prev:
            s = s + bias_ref[bi, 0, hh]
        else:
            s = s + bias_ref[bi, 0, hh, :, BAND:]
        m = jnp.max(s, axis=1, keepdims=True)
        p = jnp.exp2(s - m)
        l = jnp.sum(p, axis=1, keepdims=True)
        o = jnp.dot(p.astype(bf16), vcat, preferred_element_type=f32)
        outs.append(o * (1.0 / l))
        lses.append(m + jnp.log2(l))
    return jnp.where(lo, outs[0], outs[1]), jnp.where(lo, lses[0], lses[1])


def _attn_kernel(q1_ref, k1_ref, v1_ref, q4_ref, k4_ref, v4_ref, q16_ref, k16_ref, v16_ref,
                 bias_ref, g_ref, gmat_ref, out_ref, o_s, lse_s):
    S = q1_ref.shape[2]
    lane = lax.broadcasted_iota(jnp.int32, (BAND, LANES), 1)
    lo = lane < HEAD_DIM
    lane1 = lax.broadcasted_iota(jnp.int32, (1, LANES), 1)
    head_masks = (jnp.where(lane1 < HEAD_DIM, 1.0, 0.0).astype(bf16),
                  jnp.where(lane1 < HEAD_DIM, 0.0, 1.0).astype(bf16))
    tile = functools.partial(_attn_tile, bias_ref=bias_ref, head_masks=head_masks, lo=lo)

    def run_class(bi, d, q_ref, k_ref, v_ref, cls, r):
        nblk = S // d // BAND

        def emit(o, lse, n):
            rows = pl.ds(n * (BAND * d) + r, BAND, stride=d) if d > 1 else pl.ds(n * BAND, BAND)
            o_s[bi, rows, :] = o
            lse_s[bi, rows, :] = lse

        o, lse = tile(q_ref[cls + (pl.ds(0, BAND),)], k_ref[cls + (pl.ds(0, BAND),)],
                      v_ref[cls + (pl.ds(0, BAND),)], bi=bi, has_prev=False)
        emit(o, lse, 0)
        if nblk > 1:
            def body(n, carry):
                cur = pl.ds(pl.multiple_of(n * BAND, BAND), BAND)
                both = pl.ds(pl.multiple_of((n - 1) * BAND, BAND), 2 * BAND)
                o, lse = tile(q_ref[cls + (cur,)], k_ref[cls + (both,)], v_ref[cls + (both,)],
                              bi=bi, has_prev=True)
                emit(o, lse, n)
                return carry
            lax.fori_loop(1, nblk, body, 0)

    run_class(0, 1, q1_ref, k1_ref, v1_ref, (0, 0), 0)

    def class4(r, carry):
        run_class(1, 4, q4_ref, k4_ref, v4_ref, (0, 0, r), r)
        return carry
    lax.fori_loop(0, 4, class4, 0)

    def class16(r, carry):
        run_class(2, 16, q16_ref, k16_ref, v16_ref, (0, 0, r), r)
        return carry
    lax.fori_loop(0, 16, class16, 0)

    def merge(j, carry):
        rows = pl.ds(pl.multiple_of(j * BAND, BAND), BAND)
        l1, l4, l16 = lse_s[0, rows, :], lse_s[1, rows, :], lse_s[2, rows, :]
        lm = jnp.maximum(jnp.maximum(l1, l4), l16)
        w1, w4, w16 = jnp.exp2(l1 - lm), jnp.exp2(l4 - lm), jnp.exp2(l16 - lm)
        num = o_s[0, rows, :] * w1 + o_s[1, rows, :] * w4 + o_s[2, rows, :] * w16
        a = num / (w1 + w4 + w16)
        ss = jnp.dot((a * a).astype(bf16), gmat_ref[...], preferred_element_type=f32)
        an = a * lax.rsqrt(ss * (1.0 / HEAD_DIM) + EPS) * g_ref[0]
        out_ref[0, 0, rows, :] = an.astype(bf16)
        return carry
    lax.fori_loop(0, S // BAND, merge, 0)


def _attention(qkv1, qkv4, qkv16, bias, attn_g, gmat):
    B, _, S, _ = qkv1.shape

    def slab(which):
        return (pl.BlockSpec((1, 1, S, LANES), lambda b, p: (b, which * N_PAIR + p, 0, 0)),
                pl.BlockSpec((1, 1, 4, S // 4, LANES),
                             lambda b, p: (b, which * N_PAIR + p, 0, 0, 0)),
                pl.BlockSpec((1, 1, 16, S // 16, LANES),
                             lambda b, p: (b, which * N_PAIR + p, 0, 0, 0)))

    (q1, q4, q16), (k1, k4, k16), (v1, v4, v16) = slab(0), slab(1), slab(2)
    return pl.pallas_call(
        _attn_kernel,
        grid=(B, N_PAIR),
        in_specs=[q1, k1, v1, q4, k4, v4, q16, k16, v16,
                  pl.BlockSpec((3, 1, 2, BAND, 2 * BAND), lambda b, p: (0, p, 0, 0, 0)),
                  pl.BlockSpec((1, 1, LANES), lambda b, p: (p, 0, 0)),
                  pl.BlockSpec(gmat.shape, lambda b, p: (0, 0), pipeline_mode=pl.Buffered(1))],
        out_specs=pl.BlockSpec((1, 1, S, LANES), lambda b, p: (b, p, 0, 0)),
        out_shape=jax.ShapeDtypeStruct((B, N_PAIR, S, LANES), bf16),
        scratch_shapes=[pltpu.VMEM((3, S, LANES), f32), pltpu.VMEM((3, S, LANES), f32)],
        compiler_params=pltpu.CompilerParams(
            dimension_semantics=("arbitrary", "arbitrary"),
            vmem_limit_bytes=40 * 1024 * 1024),
        name="dilated_attention",
    )(qkv1, qkv1, qkv1, qkv4, qkv4, qkv4, qkv16, qkv16, qkv16, bias, attn_g, gmat)


def _out_ffn_kernel(a_ref, y_ref, x_ref, wo_ref, g2_ref, up_ref, cw_ref, down_ref, gf_ref,
                    out_ref, h2_s, acc_s, carry_s, *, final):
    si = pl.program_id(1)
    mix = jnp.concatenate([a_ref[0, p] for p in range(N_PAIR)] + [y_ref[0]], axis=1)
    x1 = x_ref[0] + jnp.dot(mix, wo_ref[...], preferred_element_type=f32)
    h2_s[...] = (x1 * _rms_scale(x1) * g2_ref[...]).astype(bf16)
    acc_s[...] = x1

    @pl.when(si == 0)
    def _():
        carry_s[...] = jnp.zeros_like(carry_s)

    def chunk(c, carry):
        up = jnp.dot(h2_s[...], up_ref[c], preferred_element_type=f32)
        prev = carry_s[c]
        carry_s[c] = up[TS - 8:]
        cw = cw_ref[c]
        conv = (cw[0:1] * _shift_rows(up, prev, 2) + cw[1:2] * _shift_rows(up, prev, 1)
                + cw[2:3] * up)
        gate, val = conv[:, :FF_CHUNK], conv[:, FF_CHUNK:]
        act = (gate / (1.0 + jnp.exp(-gate)) * val).astype(bf16)
        acc_s[...] += jnp.dot(act, down_ref[c], preferred_element_type=f32)
        return carry
    lax.fori_loop(0, N_FF_CHUNK, chunk, 0)

    out = acc_s[...]
    if final:
        out = out * _rms_scale(out) * gf_ref[...]
    out_ref[0] = out


def _out_ffn(attn, y, x, w_out, g2, up_r, cw_r, down_r, gf, final):
    B, S, D = x.shape
    ns = S // TS
    const = dict(pipeline_mode=pl.Buffered(1))
    return pl.pallas_call(
        functools.partial(_out_ffn_kernel, final=final),
        grid=(B, ns),
        in_specs=[
            pl.BlockSpec((1, N_PAIR, TS, LANES), lambda b, s: (b, 0, s, 0)),
            pl.BlockSpec((1, TS, D_CONV), lambda b, s: (b, s, 0)),
            pl.BlockSpec((1, TS, D), lambda b, s: (b, s, 0)),
            pl.BlockSpec(w_out.shape, lambda b, s: (0, 0), **const),
            pl.BlockSpec((1, D), lambda b, s: (0, 0), **const),
            pl.BlockSpec(up_r.shape, lambda b, s: (0, 0, 0), **const),
            pl.BlockSpec(cw_r.shape, lambda b, s: (0, 0, 0), **const),
            pl.BlockSpec(down_r.shape, lambda b, s: (0, 0, 0), **const),
            pl.BlockSpec((1, D), lambda b, s: (0, 0), **const),
        ],
        out_specs=pl.BlockSpec((1, TS, D), lambda b, s: (b, s, 0)),
        out_shape=jax.ShapeDtypeStruct((B, S, D), f32),
        scratch_shapes=[
            pltpu.VMEM((TS, D), bf16),
            pltpu.VMEM((TS, D), f32),
            pltpu.VMEM((N_FF_CHUNK, 8, 2 * FF_CHUNK), f32),
        ],
        compiler_params=pltpu.CompilerParams(
            dimension_semantics=("arbitrary", "arbitrary"),
            vmem_limit_bytes=48 * 1024 * 1024),
        name="out_ffn_final" if final else "out_ffn",
    )(attn, y, x, w_out, g2, up_r, cw_r, down_r, gf)


def _alibi_bias():
    slopes = 2.0 ** (-8.0 * jnp.arange(1, N_HEADS + 1, dtype=f32) / N_HEADS)
    i = jnp.arange(BAND)[:, None]
    j = jnp.arange(2 * BAND)[None, :]
    dist = BAND + i - j
    valid = (dist >= 0) & (dist <= BAND)
    per_d = []
    for d in DILATIONS:
        b = -(slopes * (d * LOG2E))[:, None, None] * dist.astype(f32)[None]
        per_d.append(jnp.where(valid[None], b, NEG))
    return jnp.stack(per_d).reshape(len(DILATIONS), N_PAIR, 2, BAND, 2 * BAND)


def _group_ones(n):
    g = jnp.arange(n) // HEAD_DIM
    return (g[:, None] == g[None, :]).astype(bf16)


def kernel(x, norm1_g, w_in, mix_conv_w, attn_out_g, conv_out_g, w_out, norm2_g, ffn_up,
           ffn_conv_w, ffn_down, final_norm_g):
    depth = w_in.shape[0]
    bias = _alibi_bias()
    gmat_c = _group_ones(D_CONV)
    gmat_a = _group_ones(LANES)
    gf = final_norm_g.reshape(1, D_MODEL)
    for layer in range(depth):
        up_r = (ffn_up[layer].astype(bf16).reshape(D_MODEL, 2, N_FF_CHUNK, FF_CHUNK)
                .transpose(2, 0, 1, 3).reshape(N_FF_CHUNK, D_MODEL, 2 * FF_CHUNK))
        cw_r = (ffn_conv_w[layer].reshape(3, 2, N_FF_CHUNK, FF_CHUNK)
                .transpose(2, 0, 1, 3).reshape(N_FF_CHUNK, 3, 2 * FF_CHUNK))
        down_r = ffn_down[layer].astype(bf16).reshape(N_FF_CHUNK, FF_CHUNK, D_MODEL)

        qkv1, qkv4, qkv16, y = _mixer_in(
            x, norm1_g[layer].reshape(1, D_MODEL), w_in[layer].astype(bf16), mix_conv_w[layer],
            conv_out_g[layer].reshape(1, D_CONV), gmat_c)
        attn = _attention(qkv1, qkv4, qkv16, bias,
                          attn_out_g[layer].reshape(N_PAIR, 1, LANES), gmat_a)
        x = _out_ffn(attn, y, x, w_out[layer].astype(bf16), norm2_g[layer].reshape(1, D_MODEL),
                     up_r, cw_r, down_r, gf, final=(layer == depth - 1))
    return x
```

```python
import functools
import math

import jax
import jax.numpy as jnp
from jax import lax
from jax.experimental import pallas as pl
from jax.experimental.pallas import tpu as pltpu

D_MODEL = 1024
D_ATTN = 512
D_CONV = 512
HEAD_DIM = 64
N_HEADS = D_ATTN // HEAD_DIM
DILATIONS = (1, 4, 16)
BAND = 128
D_FF = 2816
EPS = 1e-6

LANES = 128
N_SLAB = 3 * D_ATTN // LANES
N_PAIR = N_HEADS // 2
FF_CHUNK = 256
N_FF_CHUNK = D_FF // FF_CHUNK
TS = 256
LOG2E = 1.0 / math.log(2.0)
Q_SCALE = HEAD_DIM ** -0.5 * LOG2E
NEG = -1e30

f32 = jnp.float32
bf16 = jnp.bfloat16


def _rms_scale(x):
    return lax.rsqrt(jnp.mean(x * x, axis=-1, keepdims=True) + EPS)


def _shift_rows(x, prev8, k):
    rolled = pltpu.roll(x, k, axis=0)
    row = lax.broadcasted_iota(jnp.int32, prev8.shape, 0)
    head = jnp.where(row < k, pltpu.roll(prev8, k, axis=0), rolled[:8])
    return jnp.concatenate([head, rolled[8:]], axis=0)


def _mixer_in_kernel(x_ref, g_ref, w_ref, cw_ref, cg_ref, gmat_ref,
                     o1_ref, o4_ref, o16_ref, y_ref, nat_s, d4_s, carry_s):
    si = pl.program_id(1)
    x = x_ref[0]
    h = (x * _rms_scale(x) * g_ref[...]).astype(bf16)

    for j in range(N_SLAB):
        pj = jnp.dot(h, w_ref[:, j * LANES:(j + 1) * LANES], preferred_element_type=f32)
        if j < N_PAIR:
            pj = pj * Q_SCALE
        nat_s[j] = pj
        o1_ref[0, j] = pj.astype(bf16)
    for j in range(N_SLAB):
        for r in range(4):
            c4 = nat_s[j, pl.ds(r, TS // 4, stride=4), :]
            d4_s[j, r] = c4
            o4_ref[0, j, r] = c4.astype(bf16)
    for j in range(N_SLAB):
        for r in range(16):
            c16 = d4_s[j, r % 4, pl.ds(r // 4, TS // 16, stride=4), :]
            o16_ref[0, j, r] = c16.astype(bf16)

    c0 = 3 * D_ATTN
    gate_b = jnp.dot(h, w_ref[:, c0:c0 + D_CONV], preferred_element_type=f32)
    gate_c = jnp.dot(h, w_ref[:, c0 + D_CONV:c0 + 2 * D_CONV], preferred_element_type=f32)
    u = jnp.dot(h, w_ref[:, c0 + 2 * D_CONV:c0 + 3 * D_CONV], preferred_element_type=f32)
    cu = gate_c * u

    @pl.when(si == 0)
    def _():
        carry_s[...] = jnp.zeros_like(carry_s)

    prev = carry_s[...]
    carry_s[...] = cu[TS - 8:]
    cw = cw_ref[...]
    y = gate_b * (cw[0:1] * _shift_rows(cu, prev, 2) + cw[1:2] * _shift_rows(cu, prev, 1)
                  + cw[2:3] * cu)
    ss = jnp.dot((y * y).astype(bf16), gmat_ref[...], preferred_element_type=f32)
    y_ref[0] = (y * lax.rsqrt(ss * (1.0 / HEAD_DIM) + EPS) * cg_ref[...]).astype(bf16)


def _mixer_in(x, g, w_in, conv_w, conv_g, gmat):
    B, S, D = x.shape
    ns = S // TS
    const = dict(pipeline_mode=pl.Buffered(1))
    return pl.pallas_call(
        _mixer_in_kernel,
        grid=(B, ns),
        in_specs=[
            pl.BlockSpec((1, TS, D), lambda b, s: (b, s, 0)),
            pl.BlockSpec((1, D), lambda b, s: (0, 0), **const),
            pl.BlockSpec(w_in.shape, lambda b, s: (0, 0), **const),
            pl.BlockSpec(conv_w.shape, lambda b, s: (0, 0), **const),
            pl.BlockSpec((1, D_CONV), lambda b, s: (0, 0), **const),
            pl.BlockSpec(gmat.shape, lambda b, s: (0, 0), **const),
        ],
        out_specs=[
            pl.BlockSpec((1, N_SLAB, TS, LANES), lambda b, s: (b, 0, s, 0)),
            pl.BlockSpec((1, N_SLAB, 4, TS // 4, LANES), lambda b, s: (b, 0, 0, s, 0)),
            pl.BlockSpec((1, N_SLAB, 16, TS // 16, LANES), lambda b, s: (b, 0, 0, s, 0)),
            pl.BlockSpec((1, TS, D_CONV), lambda b, s: (b, s, 0)),
        ],
        out_shape=[
            jax.ShapeDtypeStruct((B, N_SLAB, S, LANES), bf16),
            jax.ShapeDtypeStruct((B, N_SLAB, 4, S // 4, LANES), bf16),
            jax.ShapeDtypeStruct((B, N_SLAB, 16, S // 16, LANES), bf16),
            jax.ShapeDtypeStruct((B, S, D_CONV), bf16),
        ],
        scratch_shapes=[
            pltpu.VMEM((N_SLAB, TS, LANES), f32),
            pltpu.VMEM((N_SLAB, 4, TS // 4, LANES), f32),
            pltpu.VMEM((8, D_CONV), f32),
        ],
        compiler_params=pltpu.CompilerParams(
            dimension_semantics=("arbitrary", "arbitrary"),
            vmem_limit_bytes=40 * 1024 * 1024),
        name="mixer_in",
    )(x, g, w_in, conv_w, conv_g, gmat)


ATTN_DEPTH = 4


def _attn_scores(q, kcat, head_masks):
    return [lax.dot_general(q * mk, kcat, (((1,), (1,)), ((), ())), preferred_element_type=f32)
            for mk in head_masks]


def _attn_finish(scores, vcat, bias, lo):
    stats = []
    for s, b in zip(scores, bias):
        s = s + b
        m = jnp.max(s, axis=1, keepdims=True)
        p = jnp.exp2(s - m)
        stats.append((p.astype(bf16), m, jnp.sum(p, axis=1, keepdims=True)))
    outs, lses = [], []
    for pb, m, l in stats:
        o = jnp.dot(pb, vcat, preferred_element_type=f32)
        outs.append(o * (1.0 / l))
        lses.append(m + jnp.log2(l))
    return jnp.where(lo, outs[0], outs[1]), jnp.where(lo, lses[0], lses[1])


def _attn_kernel(q1_ref, k1_ref, v1_ref, q4_ref, k4_ref, v4_ref, q16_ref, k16_ref, v16_ref,
                 bias_ref, g_ref, gmat_ref, out_ref, o_s, lse_s):
    S = q1_ref.shape[2]
    lane = lax.broadcasted_iota(jnp.int32, (BAND, LANES), 1)
    lo = lane < HEAD_DIM
    lane1 = lax.broadcasted_iota(jnp.int32, (1, LANES), 1)
    head_masks = (jnp.where(lane1 < HEAD_DIM, 1.0, 0.0).astype(bf16),
                  jnp.where(lane1 < HEAD_DIM, 0.0, 1.0).astype(bf16))

    blocks = []
    for bi, (d, refs) in enumerate(zip(DILATIONS, ((q1_ref, k1_ref, v1_ref),
                                                   (q4_ref, k4_ref, v4_ref),
                                                   (q16_ref, k16_ref, v16_ref)))):
        for r in range(d):
            cls = (0, 0) if d == 1 else (0, 0, r)
            blocks += [(bi, d, refs, cls, r, n) for n in range(S // d // BAND)]

    def start(blk):
        _, _, (q_ref, k_ref, _), cls, _, n = blk
        keys = pl.ds((n - 1) * BAND, 2 * BAND) if n else pl.ds(0, BAND)
        return _attn_scores(q_ref[cls + (pl.ds(n * BAND, BAND),)], k_ref[cls + (keys,)],
                            head_masks)

    def finish(blk, scores):
        bi, d, (_, _, v_ref), cls, r, n = blk
        if n:
            keys = pl.ds((n - 1) * BAND, 2 * BAND)
            bias = [bias_ref[bi, 0, hh] for hh in range(2)]
        else:
            keys = pl.ds(0, BAND)
            bias = [bias_ref[bi, 0, hh, :, BAND:] for hh in range(2)]
        o, lse = _attn_finish(scores, v_ref[cls + (keys,)], bias, lo)
        rows = pl.ds(n * (BAND * d) + r, BAND, stride=d) if d > 1 else pl.ds(n * BAND, BAND)
        o_s[bi, rows, :] = o
        lse_s[bi, rows, :] = lse

    inflight = []
    for blk in blocks:
        inflight.append((blk, start(blk)))
        if len(inflight) > ATTN_DEPTH:
            finish(*inflight.pop(0))
    while inflight:
        finish(*inflight.pop(0))

    def merge(j, carry):
        rows = pl.ds(pl.multiple_of(j * BAND, BAND), BAND)
        l1, l4, l16 = lse_s[0, rows, :], lse_s[1, rows, :], lse_s[2, rows, :]
        lm = jnp.maximum(jnp.maximum(l1, l4), l16)
        w1, w4, w16 = jnp.exp2(l1 - lm), jnp.exp2(l4 - lm), jnp.exp2(l16 - lm)
        num = o_s[0, rows, :] * w1 + o_s[1, rows, :] * w4 + o_s[2, rows, :] * w16
        a = num / (w1 + w4 + w16)
        ss = jnp.dot((a * a).astype(bf16), gmat_ref[...], preferred_element_type=f32)
        an = a * lax.rsqrt(ss * (1.0 / HEAD_DIM) + EPS) * g_ref[0]
        out_ref[0, 0, rows, :] = an.astype(bf16)
        return carry
    lax.fori_loop(0, S // BAND, merge, 0, unroll=2)


def _attention(qkv1, qkv4, qkv16, bias, attn_g, gmat):
    B, _, S, _ = qkv1.shape

    def slab(which):
        return (pl.BlockSpec((1, 1, S, LANES), lambda b, p: (b, which * N_PAIR + p, 0, 0)),
                pl.BlockSpec((1, 1, 4, S // 4, LANES),
                             lambda b, p: (b, which * N_PAIR + p, 0, 0, 0)),
                pl.BlockSpec((1, 1, 16, S // 16, LANES),
                             lambda b, p: (b, which * N_PAIR + p, 0, 0, 0)))

    (q1, q4, q16), (k1, k4, k16), (v1, v4, v16) = slab(0), slab(1), slab(2)
    return pl.pallas_call(
        _attn_kernel,
        grid=(B, N_PAIR),
        in_specs=[q1, k1, v1, q4, k4, v4, q16, k16, v16,
                  pl.BlockSpec((3, 1, 2, BAND, 2 * BAND), lambda b, p: (0, p, 0, 0, 0)),
                  pl.BlockSpec((1, 1, LANES), lambda b, p: (p, 0, 0)),
                  pl.BlockSpec(gmat.shape, lambda b, p: (0, 0), pipeline_mode=pl.Buffered(1))],
        out_specs=pl.BlockSpec((1, 1, S, LANES), lambda b, p: (b, p, 0, 0)),
        out_shape=jax.ShapeDtypeStruct((B, N_PAIR, S, LANES), bf16),
        scratch_shapes=[pltpu.VMEM((3, S, LANES), f32), pltpu.VMEM((3, S, LANES), f32)],
        compiler_params=pltpu.CompilerParams(
            dimension_semantics=("arbitrary", "arbitrary"),
            vmem_limit_bytes=40 * 1024 * 1024),
        name="dilated_attention",
    )(qkv1, qkv1, qkv1, qkv4, qkv4, qkv4, qkv16, qkv16, qkv16, bias, attn_g, gmat)


def _out_ffn_kernel(a_ref, y_ref, x_ref, wo_ref, g2_ref, up_ref, cw_ref, down_ref, gf_ref,
                    out_ref, h2_s, acc_s, carry_s, *, final):
    si = pl.program_id(1)
    mix = jnp.concatenate([a_ref[0, p] for p in range(N_PAIR)] + [y_ref[0]], axis=1)
    x1 = x_ref[0] + jnp.dot(mix, wo_ref[...], preferred_element_type=f32)
    h2_s[...] = (x1 * _rms_scale(x1) * g2_ref[...]).astype(bf16)
    acc_s[...] = x1

    @pl.when(si == 0)
    def _():
        carry_s[...] = jnp.zeros_like(carry_s)

    def up_proj(c):
        return jnp.dot(h2_s[...], up_ref[c], preferred_element_type=f32)

    up_next = up_proj(0)
    for c in range(N_FF_CHUNK):
        up = up_next
        if c + 1 < N_FF_CHUNK:
            up_next = up_proj(c + 1)
        prev = carry_s[c]
        carry_s[c] = up[TS - 8:]
        cw = cw_ref[c]
        conv = (cw[0:1] * _shift_rows(up, prev, 2) + cw[1:2] * _shift_rows(up, prev, 1)
                + cw[2:3] * up)
        gate, val = conv[:, :FF_CHUNK], conv[:, FF_CHUNK:]
        act = (gate / (1.0 + jnp.exp(-gate)) * val).astype(bf16)
        acc_s[...] += jnp.dot(act, down_ref[c], preferred_element_type=f32)

    out = acc_s[...]
    if final:
        out = out * _rms_scale(out) * gf_ref[...]
    out_ref[0] = out


def _out_ffn(attn, y, x, w_out, g2, up_r, cw_r, down_r, gf, final):
    B, S, D = x.shape
    ns = S // TS
    const = dict(pipeline_mode=pl.Buffered(1))
    return pl.pallas_call(
        functools.partial(_out_ffn_kernel, final=final),
        grid=(B, ns),
        in_specs=[
            pl.BlockSpec((1, N_PAIR, TS, LANES), lambda b, s: (b, 0, s, 0)),
            pl.BlockSpec((1, TS, D_CONV), lambda b, s: (b, s, 0)),
            pl.BlockSpec((1, TS, D), lambda b, s: (b, s, 0)),
            pl.BlockSpec(w_out.shape, lambda b, s: (0, 0), **const),
            pl.BlockSpec((1, D), lambda b, s: (0, 0), **const),
            pl.BlockSpec(up_r.shape, lambda b, s: (0, 0, 0), **const),
            pl.BlockSpec(cw_r.shape, lambda b, s: (0, 0, 0), **const),
            pl.BlockSpec(down_r.shape, lambda b, s: (0, 0, 0), **const),
            pl.BlockSpec((1, D), lambda b, s: (0, 0), **const),
        ],
        out_specs=pl.BlockSpec((1, TS, D), lambda b, s: (b, s, 0)),
        out_shape=jax.ShapeDtypeStruct((B, S, D), f32),
        scratch_shapes=[
            pltpu.VMEM((TS, D), bf16),
            pltpu.VMEM((TS, D), f32),
            pltpu.VMEM((N_FF_CHUNK, 8, 2 * FF_CHUNK), f32),
        ],
        compiler_params=pltpu.CompilerParams(
            dimension_semantics=("arbitrary", "arbitrary"),
            vmem_limit_bytes=48 * 1024 * 1024),
        name="out_ffn_final" if final else "out_ffn",
    )(attn, y, x, w_out, g2, up_r, cw_r, down_r, gf)


def _alibi_bias():
    slopes = 2.0 ** (-8.0 * jnp.arange(1, N_HEADS + 1, dtype=f32) / N_HEADS)
    i = jnp.arange(BAND)[:, None]
    j = jnp.arange(2 * BAND)[None, :]
    dist = BAND + i - j
    valid = (dist >= 0) & (dist <= BAND)
    per_d = []
    for d in DILATIONS:
        b = -(slopes * (d * LOG2E))[:, None, None] * dist.astype(f32)[None]
        per_d.append(jnp.where(valid[None], b, NEG))
    return jnp.stack(per_d).reshape(len(DILATIONS), N_PAIR, 2, BAND, 2 * BAND)


def _group_ones(n):
    g = jnp.arange(n) // HEAD_DIM
    return (g[:, None] == g[None, :]).astype(bf16)


def kernel(x, norm1_g, w_in, mix_conv_w, attn_out_g, conv_out_g, w_out, norm2_g, ffn_up,
           ffn_conv_w, ffn_down, final_norm_g):
    depth = w_in.shape[0]
    bias = _alibi_bias()
    gmat_c = _group_ones(D_CONV)
    gmat_a = _group_ones(LANES)
    gf = final_norm_g.reshape(1, D_MODEL)
    for layer in range(depth):
        up_r = (ffn_up[layer].astype(bf16).reshape(D_MODEL, 2, N_FF_CHUNK, FF_CHUNK)
                .transpose(2, 0, 1, 3).reshape(N_FF_CHUNK, D_MODEL, 2 * FF_CHUNK))
        cw_r = (ffn_conv_w[layer].reshape(3, 2, N_FF_CHUNK, FF_CHUNK)
                .transpose(2, 0, 1, 3).reshape(N_FF_CHUNK, 3, 2 * FF_CHUNK))
        down_r = ffn_down[layer].astype(bf16).reshape(N_FF_CHUNK, FF_CHUNK, D_MODEL)

        qkv1, qkv4, qkv16, y = _mixer_in(
            x, norm1_g[layer].reshape(1, D_MODEL), w_in[layer].astype(bf16), mix_conv_w[layer],
            conv_out_g[layer].reshape(1, D_CONV), gmat_c)
        attn = _attention(qkv1, qkv4, qkv16, bias,
                          attn_out_g[layer].reshape(N_PAIR, 1, LANES), gmat_a)
        x = _out_ffn(attn, y, x, w_out[layer].astype(bf16), norm2_g[layer].reshape(1, D_MODEL),
                     up_r, cw_r, down_r, gf, final=(layer == depth - 1))
    return x
```

```python
import functools
import math

import jax
import jax.numpy as jnp
from jax import lax
from jax.experimental import pallas as pl
from jax.experimental.pallas import tpu as pltpu

D_MODEL = 1024
D_ATTN = 512
D_CONV = 512
HEAD_DIM = 64
N_HEADS = D_ATTN // HEAD_DIM
DILATIONS = (1, 4, 16)
BAND = 128
D_FF = 2816
EPS = 1e-6

LANES = 128
N_SLAB = 3 * D_ATTN // LANES
N_PAIR = N_HEADS // 2
FF_CHUNK = 256
N_FF_CHUNK = D_FF // FF_CHUNK
FFN_DEPTH = 3
TS = 512
LOG2E = 1.0 / math.log(2.0)
Q_SCALE = HEAD_DIM ** -0.5 * LOG2E
NEG = -1e30

f32 = jnp.float32
bf16 = jnp.bfloat16


def _rms_scale(x):
    return lax.rsqrt(jnp.mean(x * x, axis=-1, keepdims=True) + EPS)


def _shift_rows(x, prev8, k):
    rolled = pltpu.roll(x, k, axis=0)
    row = lax.broadcasted_iota(jnp.int32, prev8.shape, 0)
    head = jnp.where(row < k, pltpu.roll(prev8, k, axis=0), rolled[:8])
    return jnp.concatenate([head, rolled[8:]], axis=0)


def _mixer_in_kernel(x_ref, g_ref, w_ref, cw_ref, cg_ref, gmat_ref,
                     o1_ref, o4_ref, o16_ref, y_ref, nat_s, d4_s, carry_s):
    si = pl.program_id(1)
    x = x_ref[0]
    h = (x * _rms_scale(x) * g_ref[...]).astype(bf16)

    for j in range(N_SLAB):
        pj = jnp.dot(h, w_ref[:, j * LANES:(j + 1) * LANES], preferred_element_type=f32)
        if j < N_PAIR:
            pj = pj * Q_SCALE
        nat_s[j] = pj
        o1_ref[0, j] = pj.astype(bf16)
    for j in range(N_SLAB):
        for r in range(4):
            c4 = nat_s[j, pl.ds(r, TS // 4, stride=4), :]
            d4_s[j, r] = c4
            o4_ref[0, j, r] = c4.astype(bf16)
    for j in range(N_SLAB):
        for r in range(16):
            c16 = d4_s[j, r % 4, pl.ds(r // 4, TS // 16, stride=4), :]
            o16_ref[0, j, r] = c16.astype(bf16)

    c0 = 3 * D_ATTN
    gate_b = jnp.dot(h, w_ref[:, c0:c0 + D_CONV], preferred_element_type=f32)
    gate_c = jnp.dot(h, w_ref[:, c0 + D_CONV:c0 + 2 * D_CONV], preferred_element_type=f32)
    u = jnp.dot(h, w_ref[:, c0 + 2 * D_CONV:c0 + 3 * D_CONV], preferred_element_type=f32)
    cu = gate_c * u

    @pl.when(si == 0)
    def _():
        carry_s[...] = jnp.zeros_like(carry_s)

    prev = carry_s[...]
    carry_s[...] = cu[TS - 8:]
    cw = cw_ref[...]
    y = gate_b * (cw[0:1] * _shift_rows(cu, prev, 2) + cw[1:2] * _shift_rows(cu, prev, 1)
                  + cw[2:3] * cu)
    ss = jnp.dot((y * y).astype(bf16), gmat_ref[...], preferred_element_type=f32)
    y_ref[0] = (y * lax.rsqrt(ss * (1.0 / HEAD_DIM) + EPS) * cg_ref[...]).astype(bf16)


def _mixer_in(x, g, w_in, conv_w, conv_g, gmat):
    B, S, D = x.shape
    ns = S // TS
    const = dict(pipeline_mode=pl.Buffered(1))
    return pl.pallas_call(
        _mixer_in_kernel,
        grid=(B, ns),
        in_specs=[
            pl.BlockSpec((1, TS, D), lambda b, s: (b, s, 0)),
            pl.BlockSpec((1, D), lambda b, s: (0, 0), **const),
            pl.BlockSpec(w_in.shape, lambda b, s: (0, 0), **const),
            pl.BlockSpec(conv_w.shape, lambda b, s: (0, 0), **const),
            pl.BlockSpec((1, D_CONV), lambda b, s: (0, 0), **const),
            pl.BlockSpec(gmat.shape, lambda b, s: (0, 0), **const),
        ],
        out_specs=[
            pl.BlockSpec((1, N_SLAB, TS, LANES), lambda b, s: (b, 0, s, 0)),
            pl.BlockSpec((1, N_SLAB, 4, TS // 4, LANES), lambda b, s: (b, 0, 0, s, 0)),
            pl.BlockSpec((1, N_SLAB, 16, TS // 16, LANES), lambda b, s: (b, 0, 0, s, 0)),
            pl.BlockSpec((1, TS, D_CONV), lambda b, s: (b, s, 0)),
        ],
        out_shape=[
            jax.ShapeDtypeStruct((B, N_SLAB, S, LANES), bf16),
            jax.ShapeDtypeStruct((B, N_SLAB, 4, S // 4, LANES), bf16),
            jax.ShapeDtypeStruct((B, N_SLAB, 16, S // 16, LANES), bf16),
            jax.ShapeDtypeStruct((B, S, D_CONV), bf16),
        ],
        scratch_shapes=[
            pltpu.VMEM((N_SLAB, TS, LANES), f32),
            pltpu.VMEM((N_SLAB, 4, TS // 4, LANES), f32),
            pltpu.VMEM((8, D_CONV), f32),
        ],
        compiler_params=pltpu.CompilerParams(
            dimension_semantics=("arbitrary", "arbitrary"),
            vmem_limit_bytes=40 * 1024 * 1024),
        name="mixer_in",
    )(x, g, w_in, conv_w, conv_g, gmat)


ATTN_DEPTH = 4
ATTN_EMIT_DEPTH = 3


def _attn_scores(q, kcat):
    qt = q.T
    zero = jnp.zeros((HEAD_DIM, BAND), qt.dtype)
    w = jnp.concatenate([jnp.concatenate([qt[:HEAD_DIM], zero], axis=0),
                         jnp.concatenate([zero, qt[HEAD_DIM:]], axis=0)], axis=1)
    return jnp.dot(kcat, w, preferred_element_type=f32)


def _attn_values(st, vt, bias):
    s = st + bias
    m = jnp.max(s, axis=0, keepdims=True)
    p = jnp.exp2(s - m)
    l = jnp.sum(p, axis=0, keepdims=True)
    ot = jnp.dot(vt, p.astype(bf16), preferred_element_type=f32)
    return ot, 1.0 / l, m + jnp.log2(l)


def _attn_rows(ot, rl, lse):
    ot = jnp.concatenate([ot[:HEAD_DIM, :BAND] * rl[:, :BAND],
                          ot[HEAD_DIM:, BAND:] * rl[:, BAND:]], axis=0)
    lset = jnp.concatenate([jnp.broadcast_to(lse[:, :BAND], (HEAD_DIM, BAND)),
                            jnp.broadcast_to(lse[:, BAND:], (HEAD_DIM, BAND))], axis=0)
    return ot.T, lset.T


def _attn_kernel(q1_ref, k1_ref, v1_ref, q4_ref, k4_ref, v4_ref, q16_ref, k16_ref, v16_ref,
                 bias_ref, g_ref, gmat_ref, out_ref, o_s, lse_s, vt_s, st_s, ot_s, stat_s):
    S = q1_ref.shape[2]

    blocks = []
    for bi, (d, refs) in enumerate(zip(DILATIONS, ((q1_ref, k1_ref, v1_ref),
                                                   (q4_ref, k4_ref, v4_ref),
                                                   (q16_ref, k16_ref, v16_ref)))):
        for r in range(d):
            cls = (0, 0) if d == 1 else (0, 0, r)
            blocks += [(bi, d, refs, cls, r, n) for n in range(S // d // BAND)]

    nblk = {d: S // d // BAND for d in DILATIONS}

    def flat(blk):
        bi, d, _, _, r, n = blk
        return bi * (S // BAND) + r * nblk[d] + n

    def start(blk, t):
        _, _, (q_ref, k_ref, v_ref), cls, _, n = blk
        rows = pl.ds(n * BAND, BAND)
        vt_s[flat(blk)] = v_ref[cls + (rows,)].T
        keys = pl.ds((n - 1) * BAND, 2 * BAND) if n else rows
        st = _attn_scores(q_ref[cls + (rows,)], k_ref[cls + (keys,)])
        st_s[t % st_s.shape[0], :st.shape[0]] = st

    def finish(blk, t):
        bi, _, _, _, _, n = blk
        slot = t % st_s.shape[0]
        if n:
            vt = jnp.concatenate([vt_s[flat(blk) - 1], vt_s[flat(blk)]], axis=1)
            ot, rl, lse = _attn_values(st_s[slot], vt, bias_ref[bi, 0])
        else:
            ot, rl, lse = _attn_values(st_s[slot, :BAND], vt_s[flat(blk)], bias_ref[bi, 0, BAND:, :])
        slot = t % ot_s.shape[0]
        ot_s[slot] = ot
        stat_s[slot, 0:1] = rl
        stat_s[slot, 1:2] = lse

    def emit(blk, t):
        bi, d, _, _, r, n = blk
        slot = t % ot_s.shape[0]
        o, lse = _attn_rows(ot_s[slot], stat_s[slot, 0:1], stat_s[slot, 1:2])
        rows = pl.ds(n * (BAND * d) + r, BAND, stride=d) if d > 1 else pl.ds(n * BAND, BAND)
        o_s[bi, rows, :] = o
        lse_s[bi, rows, :] = lse

    for t in range(len(blocks) + ATTN_DEPTH + ATTN_EMIT_DEPTH):
        if t < len(blocks):
            start(blocks[t], t)
        if 0 <= t - ATTN_DEPTH < len(blocks):
            finish(blocks[t - ATTN_DEPTH], t - ATTN_DEPTH)
        if 0 <= t - ATTN_DEPTH - ATTN_EMIT_DEPTH:
            emit(blocks[t - ATTN_DEPTH - ATTN_EMIT_DEPTH], t - ATTN_DEPTH - ATTN_EMIT_DEPTH)

    def merge(j, carry):
        rows = pl.ds(pl.multiple_of(j * BAND, BAND), BAND)
        l1, l4, l16 = lse_s[0, rows, :], lse_s[1, rows, :], lse_s[2, rows, :]
        lm = jnp.maximum(jnp.maximum(l1, l4), l16)
        w1, w4, w16 = jnp.exp2(l1 - lm), jnp.exp2(l4 - lm), jnp.exp2(l16 - lm)
        num = o_s[0, rows, :] * w1 + o_s[1, rows, :] * w4 + o_s[2, rows, :] * w16
        a = num / (w1 + w4 + w16)
        ss = jnp.dot((a * a).astype(bf16), gmat_ref[...], preferred_element_type=f32)
        an = a * lax.rsqrt(ss * (1.0 / HEAD_DIM) + EPS) * g_ref[0]
        out_ref[0, 0, rows, :] = an.astype(bf16)
        return carry
    lax.fori_loop(0, S // BAND, merge, 0, unroll=2)


def _attention(qkv1, qkv4, qkv16, bias, attn_g, gmat):
    B, _, S, _ = qkv1.shape

    def slab(which):
        return (pl.BlockSpec((1, 1, S, LANES), lambda b, p: (b, which * N_PAIR + p, 0, 0)),
                pl.BlockSpec((1, 1, 4, S // 4, LANES),
                             lambda b, p: (b, which * N_PAIR + p, 0, 0, 0)),
                pl.BlockSpec((1, 1, 16, S // 16, LANES),
                             lambda b, p: (b, which * N_PAIR + p, 0, 0, 0)))

    (q1, q4, q16), (k1, k4, k16), (v1, v4, v16) = slab(0), slab(1), slab(2)
    return pl.pallas_call(
        _attn_kernel,
        grid=(B, N_PAIR),
        in_specs=[q1, k1, v1, q4, k4, v4, q16, k16, v16,
                  pl.BlockSpec((3, 1, 2 * BAND, 2 * BAND), lambda b, p: (0, p, 0, 0)),
                  pl.BlockSpec((1, 1, LANES), lambda b, p: (p, 0, 0)),
                  pl.BlockSpec(gmat.shape, lambda b, p: (0, 0), pipeline_mode=pl.Buffered(1))],
        out_specs=pl.BlockSpec((1, 1, S, LANES), lambda b, p: (b, p, 0, 0)),
        out_shape=jax.ShapeDtypeStruct((B, N_PAIR, S, LANES), bf16),
        scratch_shapes=[pltpu.VMEM((3, S, LANES), f32), pltpu.VMEM((3, S, LANES), f32),
                        pltpu.VMEM((3 * S // BAND, LANES, BAND), bf16),
                        pltpu.VMEM((ATTN_DEPTH + 1, 2 * BAND, 2 * BAND), f32),
                        pltpu.VMEM((ATTN_EMIT_DEPTH + 1, LANES, 2 * BAND), f32),
                        pltpu.VMEM((ATTN_EMIT_DEPTH + 1, 8, 2 * BAND), f32)],
        compiler_params=pltpu.CompilerParams(
            dimension_semantics=("arbitrary", "arbitrary"),
            vmem_limit_bytes=40 * 1024 * 1024),
        name="dilated_attention",
    )(qkv1, qkv1, qkv1, qkv4, qkv4, qkv4, qkv16, qkv16, qkv16, bias, attn_g, gmat)


def _out_ffn_kernel(a_ref, y_ref, x_ref, wo_ref, g2_ref, up_ref, cw_ref, down_ref, gf_ref,
                    out_ref, h2_s, acc_s, carry_s, up_s, act_s, *, final):
    si = pl.program_id(1)
    mix = jnp.concatenate([a_ref[0, p] for p in range(N_PAIR)] + [y_ref[0]], axis=1)
    x1 = x_ref[0] + jnp.dot(mix, wo_ref[...], preferred_element_type=f32)
    h2_s[...] = (x1 * _rms_scale(x1) * g2_ref[...]).astype(bf16)
    acc_s[...] = x1

    @pl.when(si == 0)
    def _():
        carry_s[...] = jnp.zeros_like(carry_s)

    def up_proj(c):
        slot = c % FFN_DEPTH
        up_s[slot, :8] = carry_s[c]
        up_s[slot, 8:] = jnp.dot(h2_s[...], up_ref[c], preferred_element_type=f32)
        carry_s[c] = up_s[slot, TS:]

    def activate(c):
        slot = c % FFN_DEPTH
        cw = cw_ref[c]
        conv = (cw[0:1] * up_s[slot, 6:6 + TS] + cw[1:2] * up_s[slot, 7:7 + TS]
                + cw[2:3] * up_s[slot, 8:])
        gate, val = conv[:, :FF_CHUNK], conv[:, FF_CHUNK:]
        act_s[c % 2] = (gate / (1.0 + jnp.exp(-gate)) * val).astype(bf16)

    def down_proj(c):
        acc_s[...] += jnp.dot(act_s[c % 2], down_ref[c], preferred_element_type=f32)

    for t in range(N_FF_CHUNK + FFN_DEPTH):
        if t < N_FF_CHUNK:
            up_proj(t)
        if 0 <= t - (FFN_DEPTH - 1) < N_FF_CHUNK:
            activate(t - (FFN_DEPTH - 1))
        if 0 <= t - FFN_DEPTH:
            down_proj(t - FFN_DEPTH)

    out = acc_s[...]
    if final:
        out = out * _rms_scale(out) * gf_ref[...]
    out_ref[0] = out


def _out_ffn(attn, y, x, w_out, g2, up_r, cw_r, down_r, gf, final):
    B, S, D = x.shape
    ns = S // TS
    const = dict(pipeline_mode=pl.Buffered(1))
    return pl.pallas_call(
        functools.partial(_out_ffn_kernel, final=final),
        grid=(B, ns),
        in_specs=[
            pl.BlockSpec((1, N_PAIR, TS, LANES), lambda b, s: (b, 0, s, 0)),
            pl.BlockSpec((1, TS, D_CONV), lambda b, s: (b, s, 0)),
            pl.BlockSpec((1, TS, D), lambda b, s: (b, s, 0)),
            pl.BlockSpec(w_out.shape, lambda b, s: (0, 0), **const),
            pl.BlockSpec((1, D), lambda b, s: (0, 0), **const),
            pl.BlockSpec(up_r.shape, lambda b, s: (0, 0, 0), **const),
            pl.BlockSpec(cw_r.shape, lambda b, s: (0, 0, 0), **const),
            pl.BlockSpec(down_r.shape, lambda b, s: (0, 0, 0), **const),
            pl.BlockSpec((1, D), lambda b, s: (0, 0), **const),
        ],
        out_specs=pl.BlockSpec((1, TS, D), lambda b, s: (b, s, 0)),
        out_shape=jax.ShapeDtypeStruct((B, S, D), f32),
        scratch_shapes=[
            pltpu.VMEM((TS, D), bf16),
            pltpu.VMEM((TS, D), f32),
            pltpu.VMEM((N_FF_CHUNK, 8, 2 * FF_CHUNK), f32),
            pltpu.VMEM((FFN_DEPTH, 8 + TS, 2 * FF_CHUNK), f32),
            pltpu.VMEM((2, TS, FF_CHUNK), bf16),
        ],
        compiler_params=pltpu.CompilerParams(
            dimension_semantics=("arbitrary", "arbitrary"),
            vmem_limit_bytes=48 * 1024 * 1024),
        name="out_ffn_final" if final else "out_ffn",
    )(attn, y, x, w_out, g2, up_r, cw_r, down_r, gf)


def _alibi_bias():
    slopes = 2.0 ** (-8.0 * jnp.arange(1, N_HEADS + 1, dtype=f32) / N_HEADS)
    i = jnp.arange(BAND)[None, :]
    j = jnp.arange(2 * BAND)[:, None]
    dist = BAND + i - j
    valid = (dist >= 0) & (dist <= BAND)
    per_d = []
    for d in DILATIONS:
        b = -(slopes * (d * LOG2E))[:, None, None] * dist.astype(f32)[None]
        b = jnp.where(valid[None], b, NEG).reshape(N_PAIR, 2, 2 * BAND, BAND)
        per_d.append(b.transpose(0, 2, 1, 3).reshape(N_PAIR, 2 * BAND, 2 * BAND))
    return jnp.stack(per_d)


def _group_ones(n):
    g = jnp.arange(n) // HEAD_DIM
    return (g[:, None] == g[None, :]).astype(bf16)


def kernel(x, norm1_g, w_in, mix_conv_w, attn_out_g, conv_out_g, w_out, norm2_g, ffn_up,
           ffn_conv_w, ffn_down, final_norm_g):
    depth = w_in.shape[0]
    bias = _alibi_bias()
    gmat_c = _group_ones(D_CONV)
    gmat_a = _group_ones(LANES)
    gf = final_norm_g.reshape(1, D_MODEL)
    for layer in range(depth):
        up_r = (ffn_up[layer].astype(bf16).reshape(D_MODEL, 2, N_FF_CHUNK, FF_CHUNK)
                .transpose(2, 0, 1, 3).reshape(N_FF_CHUNK, D_MODEL, 2 * FF_CHUNK))
        cw_r = (ffn_conv_w[layer].reshape(3, 2, N_FF_CHUNK, FF_CHUNK)
                .transpose(2, 0, 1, 3).reshape(N_FF_CHUNK, 3, 2 * FF_CHUNK))
        down_r = ffn_down[layer].astype(bf16).reshape(N_FF_CHUNK, FF_CHUNK, D_MODEL)

        qkv1, qkv4, qkv16, y = _mixer_in(
            x, norm1_g[layer].reshape(1, D_MODEL), w_in[layer].astype(bf16), mix_conv_w[layer],
            conv_out_g[layer].reshape(1, D_CONV), gmat_c)
        attn = _attention(qkv1, qkv4, qkv16, bias,
                          attn_out_g[layer].reshape(N_PAIR, 1, LANES), gmat_a)
        x = _out_ffn(attn, y, x, w_out[layer].astype(bf16), norm2_g[layer].reshape(1, D_MODEL),
                     up_r, cw_r, down_r, gf, final=(layer == depth - 1))
    return x
```

```python
import functools
import math

import jax
import jax.numpy as jnp
from jax import lax
from jax.experimental import pallas as pl
from jax.experimental.pallas import tpu as pltpu

D_MODEL = 1024
D_ATTN = 512
D_CONV = 512
HEAD_DIM = 64
N_HEADS = D_ATTN // HEAD_DIM
DILATIONS = (1, 4, 16)
BAND = 128
D_FF = 2816
EPS = 1e-6

LANES = 128
N_SLAB = 3 * D_ATTN // LANES
N_PAIR = N_HEADS // 2
FF_CHUNK = 256
N_FF_CHUNK = D_FF // FF_CHUNK
FFN_DEPTH = 3
TS = 512
LOG2E = 1.0 / math.log(2.0)
Q_SCALE = HEAD_DIM ** -0.5 * LOG2E
NEG = -1e30

f32 = jnp.float32
bf16 = jnp.bfloat16


def _rms_scale(x):
    return lax.rsqrt(jnp.mean(x * x, axis=-1, keepdims=True) + EPS)


def _shift_rows(x, prev8, k):
    rolled = pltpu.roll(x, k, axis=0)
    row = lax.broadcasted_iota(jnp.int32, prev8.shape, 0)
    head = jnp.where(row < k, pltpu.roll(prev8, k, axis=0), rolled[:8])
    return jnp.concatenate([head, rolled[8:]], axis=0)


def _mixer_in_kernel(x_ref, g_ref, w_ref, cw_ref, cg_ref, gmat_ref,
                     o1_ref, o4_ref, o16_ref, y_ref, nat_s, d4_s, carry_s):
    si = pl.program_id(1)
    x = x_ref[0]
    h = (x * _rms_scale(x) * g_ref[...]).astype(bf16)

    for part in range(3):
        proj = jnp.dot(h, w_ref[:, part * D_ATTN:(part + 1) * D_ATTN], preferred_element_type=f32)
        if part == 0:
            proj = proj * Q_SCALE
        for p in range(N_PAIR):
            j = part * N_PAIR + p
            pj = proj[:, p * LANES:(p + 1) * LANES]
            nat_s[j] = pj
            o1_ref[0, j] = pj.astype(bf16)
    for j in range(N_SLAB):
        for r in range(4):
            c4 = nat_s[j, pl.ds(r, TS // 4, stride=4), :]
            d4_s[j, r] = c4
            o4_ref[0, j, r] = c4.astype(bf16)
    for j in range(N_SLAB):
        for r in range(16):
            c16 = d4_s[j, r % 4, pl.ds(r // 4, TS // 16, stride=4), :]
            o16_ref[0, j, r] = c16.astype(bf16)

    c0 = 3 * D_ATTN
    gate_b = jnp.dot(h, w_ref[:, c0:c0 + D_CONV], preferred_element_type=f32)
    gate_c = jnp.dot(h, w_ref[:, c0 + D_CONV:c0 + 2 * D_CONV], preferred_element_type=f32)
    u = jnp.dot(h, w_ref[:, c0 + 2 * D_CONV:c0 + 3 * D_CONV], preferred_element_type=f32)
    cu = gate_c * u

    @pl.when(si == 0)
    def _():
        carry_s[...] = jnp.zeros_like(carry_s)

    prev = carry_s[...]
    carry_s[...] = cu[TS - 8:]
    cw = cw_ref[...]
    y = gate_b * (cw[0:1] * _shift_rows(cu, prev, 2) + cw[1:2] * _shift_rows(cu, prev, 1)
                  + cw[2:3] * cu)
    ss = jnp.dot((y * y).astype(bf16), gmat_ref[...], preferred_element_type=f32)
    y_ref[0] = (y * lax.rsqrt(ss * (1.0 / HEAD_DIM) + EPS) * cg_ref[...]).astype(bf16)


def _mixer_in(x, g, w_in, conv_w, conv_g, gmat):
    B, S, D = x.shape
    ns = S // TS
    const = dict(pipeline_mode=pl.Buffered(1))
    return pl.pallas_call(
        _mixer_in_kernel,
        grid=(B, ns),
        in_specs=[
            pl.BlockSpec((1, TS, D), lambda b, s: (b, s, 0)),
            pl.BlockSpec((1, D), lambda b, s: (0, 0), **const),
            pl.BlockSpec(w_in.shape, lambda b, s: (0, 0), **const),
            pl.BlockSpec(conv_w.shape, lambda b, s: (0, 0), **const),
            pl.BlockSpec((1, D_CONV), lambda b, s: (0, 0), **const),
            pl.BlockSpec(gmat.shape, lambda b, s: (0, 0), **const),
        ],
        out_specs=[
            pl.BlockSpec((1, N_SLAB, TS, LANES), lambda b, s: (b, 0, s, 0)),
            pl.BlockSpec((1, N_SLAB, 4, TS // 4, LANES), lambda b, s: (b, 0, 0, s, 0)),
            pl.BlockSpec((1, N_SLAB, 16, TS // 16, LANES), lambda b, s: (b, 0, 0, s, 0)),
            pl.BlockSpec((1, TS, D_CONV), lambda b, s: (b, s, 0)),
        ],
        out_shape=[
            jax.ShapeDtypeStruct((B, N_SLAB, S, LANES), bf16),
            jax.ShapeDtypeStruct((B, N_SLAB, 4, S // 4, LANES), bf16),
            jax.ShapeDtypeStruct((B, N_SLAB, 16, S // 16, LANES), bf16),
            jax.ShapeDtypeStruct((B, S, D_CONV), bf16),
        ],
        scratch_shapes=[
            pltpu.VMEM((N_SLAB, TS, LANES), f32),
            pltpu.VMEM((N_SLAB, 4, TS // 4, LANES), f32),
            pltpu.VMEM((8, D_CONV), f32),
        ],
        compiler_params=pltpu.CompilerParams(
            dimension_semantics=("arbitrary", "arbitrary"),
            vmem_limit_bytes=40 * 1024 * 1024),
        name="mixer_in",
    )(x, g, w_in, conv_w, conv_g, gmat)


ATTN_DEPTH = 4
ATTN_EMIT_DEPTH = 3


def _attn_scores(q, kcat):
    qt = q.T
    zero = jnp.zeros((HEAD_DIM, BAND), qt.dtype)
    w = jnp.concatenate([jnp.concatenate([qt[:HEAD_DIM], zero], axis=0),
                         jnp.concatenate([zero, qt[HEAD_DIM:]], axis=0)], axis=1)
    return jnp.dot(kcat, w, preferred_element_type=f32)


def _attn_values(st, vt, bias):
    s = st + bias
    m = jnp.max(s, axis=0, keepdims=True)
    p = jnp.exp2(s - m)
    l = jnp.sum(p, axis=0, keepdims=True)
    ot = jnp.dot(vt, p.astype(bf16), preferred_element_type=f32)
    return ot, 1.0 / l, m + jnp.log2(l)


def _attn_rows(ot, rl, lse):
    ot = jnp.concatenate([ot[:HEAD_DIM, :BAND] * rl[:, :BAND],
                          ot[HEAD_DIM:, BAND:] * rl[:, BAND:]], axis=0)
    lset = jnp.concatenate([jnp.broadcast_to(lse[:, :BAND], (HEAD_DIM, BAND)),
                            jnp.broadcast_to(lse[:, BAND:], (HEAD_DIM, BAND))], axis=0)
    return ot.T, lset.T


def _attn_kernel(q1_ref, k1_ref, v1_ref, q4_ref, k4_ref, v4_ref, q16_ref, k16_ref, v16_ref,
                 bias_ref, g_ref, gmat_ref, out_ref, o_s, lse_s, vt_s, st_s, ot_s, stat_s):
    S = q1_ref.shape[2]
    pair = pl.program_id(1)

    blocks = []
    for bi, (d, refs) in enumerate(zip(DILATIONS, ((q1_ref, k1_ref, v1_ref),
                                                   (q4_ref, k4_ref, v4_ref),
                                                   (q16_ref, k16_ref, v16_ref)))):
        for r in range(d):
            cls = (0, 0) if d == 1 else (0, 0, r)
            blocks += [(bi, d, refs, cls, r, n) for n in range(S // d // BAND)]

    nblk = {d: S // d // BAND for d in DILATIONS}

    def flat(blk):
        bi, d, _, _, r, n = blk
        return bi * (S // BAND) + r * nblk[d] + n

    def start(blk, t):
        _, _, (q_ref, k_ref, v_ref), cls, _, n = blk
        rows = pl.ds(n * BAND, BAND)
        vt_s[flat(blk)] = v_ref[cls + (rows,)].T
        keys = pl.ds((n - 1) * BAND, 2 * BAND) if n else rows
        st = _attn_scores(q_ref[cls + (rows,)], k_ref[cls + (keys,)])
        st_s[t % st_s.shape[0], :st.shape[0]] = st

    def finish(blk, t):
        bi, _, _, _, _, n = blk
        slot = t % st_s.shape[0]
        if n:
            vt = jnp.concatenate([vt_s[flat(blk) - 1], vt_s[flat(blk)]], axis=1)
            ot, rl, lse = _attn_values(st_s[slot], vt, bias_ref[bi, pair])
        else:
            ot, rl, lse = _attn_values(st_s[slot, :BAND], vt_s[flat(blk)],
                                       bias_ref[bi, pair, BAND:, :])
        slot = t % ot_s.shape[0]
        ot_s[slot] = ot
        stat_s[slot, 0:1] = rl
        stat_s[slot, 1:2] = lse

    def emit(blk, t):
        bi, d, _, _, r, n = blk
        slot = t % ot_s.shape[0]
        o, lse = _attn_rows(ot_s[slot], stat_s[slot, 0:1], stat_s[slot, 1:2])
        rows = pl.ds(n * (BAND * d) + r, BAND, stride=d) if d > 1 else pl.ds(n * BAND, BAND)
        o_s[bi, rows, :] = o
        lse_s[bi, rows, :] = lse

    for t in range(len(blocks) + ATTN_DEPTH + ATTN_EMIT_DEPTH):
        if t < len(blocks):
            start(blocks[t], t)
        if 0 <= t - ATTN_DEPTH < len(blocks):
            finish(blocks[t - ATTN_DEPTH], t - ATTN_DEPTH)
        if 0 <= t - ATTN_DEPTH - ATTN_EMIT_DEPTH:
            emit(blocks[t - ATTN_DEPTH - ATTN_EMIT_DEPTH], t - ATTN_DEPTH - ATTN_EMIT_DEPTH)

    def merge(j, carry):
        rows = pl.ds(pl.multiple_of(j * BAND, BAND), BAND)
        l1, l4, l16 = lse_s[0, rows, :], lse_s[1, rows, :], lse_s[2, rows, :]
        lm = jnp.maximum(jnp.maximum(l1, l4), l16)
        w1, w4, w16 = jnp.exp2(l1 - lm), jnp.exp2(l4 - lm), jnp.exp2(l16 - lm)
        num = o_s[0, rows, :] * w1 + o_s[1, rows, :] * w4 + o_s[2, rows, :] * w16
        a = num / (w1 + w4 + w16)
        ss = jnp.dot((a * a).astype(bf16), gmat_ref[...], preferred_element_type=f32)
        an = a * lax.rsqrt(ss * (1.0 / HEAD_DIM) + EPS) * g_ref[0]
        out_ref[0, 0, rows, :] = an.astype(bf16)
        return carry
    lax.fori_loop(0, S // BAND, merge, 0, unroll=2)


def _attention(qkv1, qkv4, qkv16, bias, attn_g, gmat):
    B, _, S, _ = qkv1.shape

    def slab(which):
        return (pl.BlockSpec((1, 1, S, LANES), lambda b, p: (b, which * N_PAIR + p, 0, 0)),
                pl.BlockSpec((1, 1, 4, S // 4, LANES),
                             lambda b, p: (b, which * N_PAIR + p, 0, 0, 0)),
                pl.BlockSpec((1, 1, 16, S // 16, LANES),
                             lambda b, p: (b, which * N_PAIR + p, 0, 0, 0)))

    (q1, q4, q16), (k1, k4, k16), (v1, v4, v16) = slab(0), slab(1), slab(2)
    return pl.pallas_call(
        _attn_kernel,
        grid=(B, N_PAIR),
        in_specs=[q1, k1, v1, q4, k4, v4, q16, k16, v16,
                  pl.BlockSpec(bias.shape, lambda b, p: (0, 0, 0, 0), pipeline_mode=pl.Buffered(1)),
                  pl.BlockSpec((1, 1, LANES), lambda b, p: (p, 0, 0)),
                  pl.BlockSpec(gmat.shape, lambda b, p: (0, 0), pipeline_mode=pl.Buffered(1))],
        out_specs=pl.BlockSpec((1, 1, S, LANES), lambda b, p: (b, p, 0, 0)),
        out_shape=jax.ShapeDtypeStruct((B, N_PAIR, S, LANES), bf16),
        scratch_shapes=[pltpu.VMEM((3, S, LANES), f32), pltpu.VMEM((3, S, LANES), f32),
                        pltpu.VMEM((3 * S // BAND, LANES, BAND), bf16),
                        pltpu.VMEM((ATTN_DEPTH + 1, 2 * BAND, 2 * BAND), f32),
                        pltpu.VMEM((ATTN_EMIT_DEPTH + 1, LANES, 2 * BAND), f32),
                        pltpu.VMEM((ATTN_EMIT_DEPTH + 1, 8, 2 * BAND), f32)],
        compiler_params=pltpu.CompilerParams(
            dimension_semantics=("arbitrary", "arbitrary"),
            vmem_limit_bytes=40 * 1024 * 1024),
        name="dilated_attention",
    )(qkv1, qkv1, qkv1, qkv4, qkv4, qkv4, qkv16, qkv16, qkv16, bias, attn_g, gmat)


def _out_ffn_kernel(a_ref, y_ref, x_ref, wo_ref, g2_ref, up_ref, cw_ref, down_ref, gf_ref,
                    out_ref, h2_s, acc_s, carry_s, *rings, final):
    up_slots, act_slots = rings[:FFN_DEPTH], rings[FFN_DEPTH:]
    si = pl.program_id(1)
    mix = jnp.concatenate([a_ref[0, p] for p in range(N_PAIR)] + [y_ref[0]], axis=1)
    x1 = x_ref[0] + jnp.dot(mix, wo_ref[...], preferred_element_type=f32)
    h2_s[...] = (x1 * _rms_scale(x1) * g2_ref[...]).astype(bf16)
    acc_s[...] = x1

    @pl.when(si == 0)
    def _():
        carry_s[...] = jnp.zeros_like(carry_s)

    def halves(c):
        return [pl.ds(half * D_FF + c * FF_CHUNK, FF_CHUNK) for half in range(2)]

    def up_proj(c):
        up_s = up_slots[c % FFN_DEPTH]
        up_s[:8] = carry_s[c]
        for half, cols in enumerate(halves(c)):
            up_s[8:, half * FF_CHUNK:(half + 1) * FF_CHUNK] = jnp.dot(
                h2_s[...], up_ref[:, cols], preferred_element_type=f32)
        carry_s[c] = up_s[TS:]

    def activate(c):
        up_s = up_slots[c % FFN_DEPTH]
        cw = jnp.concatenate([cw_ref[:, cols] for cols in halves(c)], axis=1)
        conv = cw[0:1] * up_s[6:6 + TS] + cw[1:2] * up_s[7:7 + TS] + cw[2:3] * up_s[8:]
        gate, val = conv[:, :FF_CHUNK], conv[:, FF_CHUNK:]
        act_slots[c % 2][...] = (gate / (1.0 + jnp.exp(-gate)) * val).astype(bf16)

    def down_proj(c):
        acc_s[...] += jnp.dot(act_slots[c % 2][...], down_ref[pl.ds(c * FF_CHUNK, FF_CHUNK), :],
                              preferred_element_type=f32)

    for t in range(N_FF_CHUNK + FFN_DEPTH):
        if t < N_FF_CHUNK:
            up_proj(t)
        if 0 <= t - (FFN_DEPTH - 1) < N_FF_CHUNK:
            activate(t - (FFN_DEPTH - 1))
        if 0 <= t - FFN_DEPTH:
            down_proj(t - FFN_DEPTH)

    out = acc_s[...]
    if final:
        out = out * _rms_scale(out) * gf_ref[...]
    out_ref[0] = out


def _out_ffn(attn, y, x, w_out, g2, up_w, conv_w, down_w, gf, final):
    B, S, D = x.shape
    ns = S // TS
    const = dict(pipeline_mode=pl.Buffered(1))
    return pl.pallas_call(
        functools.partial(_out_ffn_kernel, final=final),
        grid=(B, ns),
        in_specs=[
            pl.BlockSpec((1, N_PAIR, TS, LANES), lambda b, s: (b, 0, s, 0)),
            pl.BlockSpec((1, TS, D_CONV), lambda b, s: (b, s, 0)),
            pl.BlockSpec((1, TS, D), lambda b, s: (b, s, 0)),
            pl.BlockSpec(w_out.shape, lambda b, s: (0, 0), **const),
            pl.BlockSpec((1, D), lambda b, s: (0, 0), **const),
            pl.BlockSpec(up_w.shape, lambda b, s: (0, 0), **const),
            pl.BlockSpec(conv_w.shape, lambda b, s: (0, 0), **const),
            pl.BlockSpec(down_w.shape, lambda b, s: (0, 0), **const),
            pl.BlockSpec((1, D), lambda b, s: (0, 0), **const),
        ],
        out_specs=pl.BlockSpec((1, TS, D), lambda b, s: (b, s, 0)),
        out_shape=jax.ShapeDtypeStruct((B, S, D), f32),
        scratch_shapes=[
            pltpu.VMEM((TS, D), bf16),
            pltpu.VMEM((TS, D), f32),
            pltpu.VMEM((N_FF_CHUNK, 8, 2 * FF_CHUNK), f32),
        ] + [pltpu.VMEM((8 + TS, 2 * FF_CHUNK), f32)] * FFN_DEPTH + [
            pltpu.VMEM((TS, FF_CHUNK), bf16)] * 2 + [
        ],
        compiler_params=pltpu.CompilerParams(
            dimension_semantics=("arbitrary", "arbitrary"),
            vmem_limit_bytes=48 * 1024 * 1024),
        name="out_ffn_final" if final else "out_ffn",
    )(attn, y, x, w_out, g2, up_w, conv_w, down_w, gf)


def _alibi_bias():
    slopes = 2.0 ** (-8.0 * jnp.arange(1, N_HEADS + 1, dtype=f32) / N_HEADS)
    i = jnp.arange(BAND)[None, :]
    j = jnp.arange(2 * BAND)[:, None]
    dist = BAND + i - j
    valid = (dist >= 0) & (dist <= BAND)
    per_d = []
    for d in DILATIONS:
        b = -(slopes * (d * LOG2E))[:, None, None] * dist.astype(f32)[None]
        b = jnp.where(valid[None], b, NEG).reshape(N_PAIR, 2, 2 * BAND, BAND)
        per_d.append(b.transpose(0, 2, 1, 3).reshape(N_PAIR, 2 * BAND, 2 * BAND))
    return jnp.stack(per_d)


def _group_ones(n):
    g = jnp.arange(n) // HEAD_DIM
    return (g[:, None] == g[None, :]).astype(bf16)


def kernel(x, norm1_g, w_in, mix_conv_w, attn_out_g, conv_out_g, w_out, norm2_g, ffn_up,
           ffn_conv_w, ffn_down, final_norm_g):
    depth = w_in.shape[0]
    bias = _alibi_bias()
    gmat_c = _group_ones(D_CONV)
    gmat_a = _group_ones(LANES)
    gf = final_norm_g.reshape(1, D_MODEL)
    for layer in range(depth):
        qkv1, qkv4, qkv16, y = _mixer_in(
            x, norm1_g[layer].reshape(1, D_MODEL), w_in[layer].astype(bf16), mix_conv_w[layer],
            conv_out_g[layer].reshape(1, D_CONV), gmat_c)
        attn = _attention(qkv1, qkv4, qkv16, bias,
                          attn_out_g[layer].reshape(N_PAIR, 1, LANES), gmat_a)
        x = _out_ffn(attn, y, x, w_out[layer].astype(bf16), norm2_g[layer].reshape(1, D_MODEL),
                     ffn_up[layer].astype(bf16), ffn_conv_w[layer], ffn_down[layer].astype(bf16), gf,
                     final=(layer == depth - 1))
    return x
```

```python
import functools
import math

import jax
import jax.numpy as jnp
from jax import lax
from jax.experimental import pallas as pl
from jax.experimental.pallas import tpu as pltpu

D_MODEL = 1024
D_ATTN = 512
D_CONV = 512
HEAD_DIM = 64
N_HEADS = D_ATTN // HEAD_DIM
DILATIONS = (1, 4, 16)
BAND = 128
D_FF = 2816
EPS = 1e-6

LANES = 128
N_SLAB = 3 * D_ATTN // LANES
N_PAIR = N_HEADS // 2
FF_CHUNK = 256
N_FF_CHUNK = D_FF // FF_CHUNK
FFN_DEPTH = 3
TS = 512
TS_FFN = 256
LOG2E = 1.0 / math.log(2.0)
Q_SCALE = HEAD_DIM ** -0.5 * LOG2E
NEG = -1e30

f32 = jnp.float32
bf16 = jnp.bfloat16


def _rms_scale(x):
    return lax.rsqrt(jnp.mean(x * x, axis=-1, keepdims=True) + EPS)


def _shift_rows(x, prev8, k):
    rolled = pltpu.roll(x, k, axis=0)
    row = lax.broadcasted_iota(jnp.int32, prev8.shape, 0)
    head = jnp.where(row < k, pltpu.roll(prev8, k, axis=0), rolled[:8])
    return jnp.concatenate([head, rolled[8:]], axis=0)


def _mixer_in_kernel(x_ref, g_ref, w_ref, cw_ref, cg_ref, gmat_ref,
                     o1_ref, o4_ref, o16_ref, y_ref, nat_s, d4_s, carry_s):
    si = pl.program_id(1)
    x = x_ref[0]
    h = (x * _rms_scale(x) * g_ref[...]).astype(bf16)

    for part in range(3):
        proj = jnp.dot(h, w_ref[:, part * D_ATTN:(part + 1) * D_ATTN], preferred_element_type=f32)
        if part == 0:
            proj = proj * Q_SCALE
        for p in range(N_PAIR):
            j = part * N_PAIR + p
            pj = proj[:, p * LANES:(p + 1) * LANES]
            nat_s[j] = pj
            o1_ref[0, j] = pj.astype(bf16)
    for j in range(N_SLAB):
        for r in range(4):
            c4 = nat_s[j, pl.ds(r, TS // 4, stride=4), :]
            d4_s[j, r] = c4
            o4_ref[0, j, r] = c4.astype(bf16)
    for j in range(N_SLAB):
        for r in range(16):
            c16 = d4_s[j, r % 4, pl.ds(r // 4, TS // 16, stride=4), :]
            o16_ref[0, j, r] = c16.astype(bf16)

    c0 = 3 * D_ATTN
    gate_b = jnp.dot(h, w_ref[:, c0:c0 + D_CONV], preferred_element_type=f32)
    gate_c = jnp.dot(h, w_ref[:, c0 + D_CONV:c0 + 2 * D_CONV], preferred_element_type=f32)
    u = jnp.dot(h, w_ref[:, c0 + 2 * D_CONV:c0 + 3 * D_CONV], preferred_element_type=f32)
    cu = gate_c * u

    @pl.when(si == 0)
    def _():
        carry_s[...] = jnp.zeros_like(carry_s)

    prev = carry_s[...]
    carry_s[...] = cu[TS - 8:]
    cw = cw_ref[...]
    y = gate_b * (cw[0:1] * _shift_rows(cu, prev, 2) + cw[1:2] * _shift_rows(cu, prev, 1)
                  + cw[2:3] * cu)
    ss = jnp.dot((y * y).astype(bf16), gmat_ref[...], preferred_element_type=f32)
    y_ref[0] = (y * lax.rsqrt(ss * (1.0 / HEAD_DIM) + EPS) * cg_ref[...]).astype(bf16)


def _mixer_in(x, g, w_in, conv_w, conv_g, gmat):
    B, S, D = x.shape
    ns = S // TS
    const = dict(pipeline_mode=pl.Buffered(1))
    return pl.pallas_call(
        _mixer_in_kernel,
        grid=(B, ns),
        in_specs=[
            pl.BlockSpec((1, TS, D), lambda b, s: (b, s, 0)),
            pl.BlockSpec((1, D), lambda b, s: (0, 0), **const),
            pl.BlockSpec(w_in.shape, lambda b, s: (0, 0), **const),
            pl.BlockSpec(conv_w.shape, lambda b, s: (0, 0), **const),
            pl.BlockSpec((1, D_CONV), lambda b, s: (0, 0), **const),
            pl.BlockSpec(gmat.shape, lambda b, s: (0, 0), **const),
        ],
        out_specs=[
            pl.BlockSpec((1, N_SLAB, TS, LANES), lambda b, s: (b, 0, s, 0)),
            pl.BlockSpec((1, N_SLAB, 4, TS // 4, LANES), lambda b, s: (b, 0, 0, s, 0)),
            pl.BlockSpec((1, N_SLAB, 16, TS // 16, LANES), lambda b, s: (b, 0, 0, s, 0)),
            pl.BlockSpec((1, TS, D_CONV), lambda b, s: (b, s, 0)),
        ],
        out_shape=[
            jax.ShapeDtypeStruct((B, N_SLAB, S, LANES), bf16),
            jax.ShapeDtypeStruct((B, N_SLAB, 4, S // 4, LANES), bf16),
            jax.ShapeDtypeStruct((B, N_SLAB, 16, S // 16, LANES), bf16),
            jax.ShapeDtypeStruct((B, S, D_CONV), bf16),
        ],
        scratch_shapes=[
            pltpu.VMEM((N_SLAB, TS, LANES), f32),
            pltpu.VMEM((N_SLAB, 4, TS // 4, LANES), f32),
            pltpu.VMEM((8, D_CONV), f32),
        ],
        compiler_params=pltpu.CompilerParams(
            dimension_semantics=("arbitrary", "arbitrary"),
            vmem_limit_bytes=40 * 1024 * 1024),
        name="mixer_in",
    )(x, g, w_in, conv_w, conv_g, gmat)


ATTN_DEPTH = 4
ATTN_EMIT_DEPTH = 3


def _attn_scores(q, kcat):
    qt = q.T
    zero = jnp.zeros((HEAD_DIM, BAND), qt.dtype)
    w = jnp.concatenate([jnp.concatenate([qt[:HEAD_DIM], zero], axis=0),
                         jnp.concatenate([zero, qt[HEAD_DIM:]], axis=0)], axis=1)
    return jnp.dot(kcat, w, preferred_element_type=f32)


def _attn_values(st, vt, bias):
    s = st + bias
    m = jnp.max(s, axis=0, keepdims=True)
    p = jnp.exp2(s - m)
    l = jnp.sum(p, axis=0, keepdims=True)
    pb = p.astype(bf16)
    ot = jnp.concatenate([jnp.dot(vt[:HEAD_DIM], pb[:, :BAND], preferred_element_type=f32),
                          jnp.dot(vt[HEAD_DIM:], pb[:, BAND:], preferred_element_type=f32)], axis=0)
    return ot, 1.0 / l, m + jnp.log2(l)


def _attn_rows(ot, rl, lse):
    ot = jnp.concatenate([ot[:HEAD_DIM] * rl[:, :BAND], ot[HEAD_DIM:] * rl[:, BAND:]], axis=0)
    lset = jnp.concatenate([jnp.broadcast_to(lse[:, :BAND], (HEAD_DIM, BAND)),
                            jnp.broadcast_to(lse[:, BAND:], (HEAD_DIM, BAND))], axis=0)
    return ot.T, lset.T


def _attn_kernel(q1_ref, k1_ref, v1_ref, q4_ref, k4_ref, v4_ref, q16_ref, k16_ref, v16_ref,
                 bias_ref, g_ref, gmat_ref, out_ref, o_s, lse_s, vt_s, st_s, ot_s, stat_s):
    S = q1_ref.shape[2]
    pair = pl.program_id(1)

    blocks = []
    for bi, (d, refs) in enumerate(zip(DILATIONS, ((q1_ref, k1_ref, v1_ref),
                                                   (q4_ref, k4_ref, v4_ref),
                                                   (q16_ref, k16_ref, v16_ref)))):
        for r in range(d):
            cls = (0, 0) if d == 1 else (0, 0, r)
            blocks += [(bi, d, refs, cls, r, n) for n in range(S // d // BAND)]

    nblk = {d: S // d // BAND for d in DILATIONS}

    def flat(blk):
        bi, d, _, _, r, n = blk
        return bi * (S // BAND) + r * nblk[d] + n

    def start(blk, t):
        _, _, (q_ref, k_ref, v_ref), cls, _, n = blk
        rows = pl.ds(n * BAND, BAND)
        vt_s[flat(blk)] = v_ref[cls + (rows,)].T
        keys = pl.ds((n - 1) * BAND, 2 * BAND) if n else rows
        st = _attn_scores(q_ref[cls + (rows,)], k_ref[cls + (keys,)])
        st_s[t % st_s.shape[0], :st.shape[0]] = st

    def finish(blk, t):
        bi, _, _, _, _, n = blk
        slot = t % st_s.shape[0]
        if n:
            vt = jnp.concatenate([vt_s[flat(blk) - 1], vt_s[flat(blk)]], axis=1)
            ot, rl, lse = _attn_values(st_s[slot], vt, bias_ref[bi, pair])
        else:
            ot, rl, lse = _attn_values(st_s[slot, :BAND], vt_s[flat(blk)],
                                       bias_ref[bi, pair, BAND:, :])
        slot = t % ot_s.shape[0]
        ot_s[slot] = ot
        stat_s[slot, 0:1] = rl
        stat_s[slot, 1:2] = lse

    def emit(blk, t):
        bi, d, _, _, r, n = blk
        slot = t % ot_s.shape[0]
        o, lse = _attn_rows(ot_s[slot], stat_s[slot, 0:1], stat_s[slot, 1:2])
        rows = pl.ds(n * (BAND * d) + r, BAND, stride=d) if d > 1 else pl.ds(n * BAND, BAND)
        o_s[bi, rows, :] = o
        lse_s[bi, rows, :] = lse

    for t in range(len(blocks) + ATTN_DEPTH + ATTN_EMIT_DEPTH):
        if t < len(blocks):
            start(blocks[t], t)
        if 0 <= t - ATTN_DEPTH < len(blocks):
            finish(blocks[t - ATTN_DEPTH], t - ATTN_DEPTH)
        if 0 <= t - ATTN_DEPTH - ATTN_EMIT_DEPTH:
            emit(blocks[t - ATTN_DEPTH - ATTN_EMIT_DEPTH], t - ATTN_DEPTH - ATTN_EMIT_DEPTH)

    def merge(j, carry):
        rows = pl.ds(pl.multiple_of(j * BAND, BAND), BAND)
        l1, l4, l16 = lse_s[0, rows, :], lse_s[1, rows, :], lse_s[2, rows, :]
        lm = jnp.maximum(jnp.maximum(l1, l4), l16)
        w1, w4, w16 = jnp.exp2(l1 - lm), jnp.exp2(l4 - lm), jnp.exp2(l16 - lm)
        num = o_s[0, rows, :] * w1 + o_s[1, rows, :] * w4 + o_s[2, rows, :] * w16
        a = num / (w1 + w4 + w16)
        ss = jnp.dot((a * a).astype(bf16), gmat_ref[...], preferred_element_type=f32)
        an = a * lax.rsqrt(ss * (1.0 / HEAD_DIM) + EPS) * g_ref[0]
        out_ref[0, 0, rows, :] = an.astype(bf16)
        return carry
    lax.fori_loop(0, S // BAND, merge, 0, unroll=2)


def _attention(qkv1, qkv4, qkv16, bias, attn_g, gmat):
    B, _, S, _ = qkv1.shape

    def slab(which):
        return (pl.BlockSpec((1, 1, S, LANES), lambda b, p: (b, which * N_PAIR + p, 0, 0)),
                pl.BlockSpec((1, 1, 4, S // 4, LANES),
                             lambda b, p: (b, which * N_PAIR + p, 0, 0, 0)),
                pl.BlockSpec((1, 1, 16, S // 16, LANES),
                             lambda b, p: (b, which * N_PAIR + p, 0, 0, 0)))

    (q1, q4, q16), (k1, k4, k16), (v1, v4, v16) = slab(0), slab(1), slab(2)
    return pl.pallas_call(
        _attn_kernel,
        grid=(B, N_PAIR),
        in_specs=[q1, k1, v1, q4, k4, v4, q16, k16, v16,
                  pl.BlockSpec(bias.shape, lambda b, p: (0, 0, 0, 0), pipeline_mode=pl.Buffered(1)),
                  pl.BlockSpec((1, 1, LANES), lambda b, p: (p, 0, 0)),
                  pl.BlockSpec(gmat.shape, lambda b, p: (0, 0), pipeline_mode=pl.Buffered(1))],
        out_specs=pl.BlockSpec((1, 1, S, LANES), lambda b, p: (b, p, 0, 0)),
        out_shape=jax.ShapeDtypeStruct((B, N_PAIR, S, LANES), bf16),
        scratch_shapes=[pltpu.VMEM((3, S, LANES), f32), pltpu.VMEM((3, S, LANES), f32),
                        pltpu.VMEM((3 * S // BAND, LANES, BAND), bf16),
                        pltpu.VMEM((ATTN_DEPTH + 1, 2 * BAND, 2 * BAND), f32),
                        pltpu.VMEM((ATTN_EMIT_DEPTH + 1, LANES, BAND), f32),
                        pltpu.VMEM((ATTN_EMIT_DEPTH + 1, 8, 2 * BAND), f32)],
        compiler_params=pltpu.CompilerParams(
            dimension_semantics=("arbitrary", "arbitrary"),
            vmem_limit_bytes=40 * 1024 * 1024),
        name="dilated_attention",
    )(qkv1, qkv1, qkv1, qkv4, qkv4, qkv4, qkv16, qkv16, qkv16, bias, attn_g, gmat)


def _out_ffn_kernel(a_ref, y_ref, x_ref, wo_ref, g2_ref, up_ref, cw_ref, down_ref, gf_ref,
                    out_ref, h2_s, acc_s, carry_s, *rings, final):
    up_slots, act_slots = rings[:FFN_DEPTH], rings[FFN_DEPTH:]
    si = pl.program_id(1)
    mix = jnp.concatenate([a_ref[0, p] for p in range(N_PAIR)] + [y_ref[0]], axis=1)
    x1 = x_ref[0] + jnp.dot(mix, wo_ref[...], preferred_element_type=f32)
    h2_s[...] = (x1 * _rms_scale(x1) * g2_ref[...]).astype(bf16)
    acc_s[...] = x1
    down = []

    @pl.when(si == 0)
    def _():
        carry_s[...] = jnp.zeros_like(carry_s)

    def halves(c):
        return [pl.ds(half * D_FF + c * FF_CHUNK, FF_CHUNK) for half in range(2)]

    def up_proj(c):
        up_s = up_slots[c % FFN_DEPTH]
        up_s[:8] = carry_s[c]
        for half, cols in enumerate(halves(c)):
            up_s[8:, half * FF_CHUNK:(half + 1) * FF_CHUNK] = jnp.dot(
                h2_s[...], up_ref[:, cols], preferred_element_type=f32)
        carry_s[c] = up_s[TS_FFN:]

    def activate(c):
        up_s = up_slots[c % FFN_DEPTH]
        cw = jnp.concatenate([cw_ref[:, cols] for cols in halves(c)], axis=1)
        conv = cw[0:1] * up_s[6:6 + TS_FFN] + cw[1:2] * up_s[7:7 + TS_FFN] + cw[2:3] * up_s[8:]
        gate, val = conv[:, :FF_CHUNK], conv[:, FF_CHUNK:]
        act_slots[c % 2][...] = (gate / (1.0 + jnp.exp(-gate)) * val).astype(bf16)

    def down_proj(c):
        down.append(jnp.dot(act_slots[c % 2][...], down_ref[pl.ds(c * FF_CHUNK, FF_CHUNK), :],
                            preferred_element_type=f32))

    for t in range(N_FF_CHUNK + FFN_DEPTH):
        if t < N_FF_CHUNK:
            up_proj(t)
        if 0 <= t - (FFN_DEPTH - 1) < N_FF_CHUNK:
            activate(t - (FFN_DEPTH - 1))
        if 0 <= t - FFN_DEPTH:
            down_proj(t - FFN_DEPTH)

    out = acc_s[...] + functools.reduce(lambda u, v: u + v, down)
    if final:
        out = out * _rms_scale(out) * gf_ref[...]
    out_ref[0] = out


def _out_ffn(attn, y, x, w_out, g2, up_w, conv_w, down_w, gf, final):
    B, S, D = x.shape
    ns = S // TS_FFN
    const = dict(pipeline_mode=pl.Buffered(1))
    return pl.pallas_call(
        functools.partial(_out_ffn_kernel, final=final),
        grid=(B, ns),
        in_specs=[
            pl.BlockSpec((1, N_PAIR, TS_FFN, LANES), lambda b, s: (b, 0, s, 0)),
            pl.BlockSpec((1, TS_FFN, D_CONV), lambda b, s: (b, s, 0)),
            pl.BlockSpec((1, TS_FFN, D), lambda b, s: (b, s, 0)),
            pl.BlockSpec(w_out.shape, lambda b, s: (0, 0), **const),
            pl.BlockSpec((1, D), lambda b, s: (0, 0), **const),
            pl.BlockSpec(up_w.shape, lambda b, s: (0, 0), **const),
            pl.BlockSpec(conv_w.shape, lambda b, s: (0, 0), **const),
            pl.BlockSpec(down_w.shape, lambda b, s: (0, 0), **const),
            pl.BlockSpec((1, D), lambda b, s: (0, 0), **const),
        ],
        out_specs=pl.BlockSpec((1, TS_FFN, D), lambda b, s: (b, s, 0)),
        out_shape=jax.ShapeDtypeStruct((B, S, D), f32),
        scratch_shapes=[
            pltpu.VMEM((TS_FFN, D), bf16),
            pltpu.VMEM((TS_FFN, D), f32),
            pltpu.VMEM((N_FF_CHUNK, 8, 2 * FF_CHUNK), f32),
        ] + [pltpu.VMEM((8 + TS_FFN, 2 * FF_CHUNK), f32)] * FFN_DEPTH + [
            pltpu.VMEM((TS_FFN, FF_CHUNK), bf16)] * 2 + [
        ],
        compiler_params=pltpu.CompilerParams(
            dimension_semantics=("arbitrary", "arbitrary"),
            vmem_limit_bytes=48 * 1024 * 1024),
        name="out_ffn_final" if final else "out_ffn",
    )(attn, y, x, w_out, g2, up_w, conv_w, down_w, gf)


def _alibi_bias():
    slopes = 2.0 ** (-8.0 * jnp.arange(1, N_HEADS + 1, dtype=f32) / N_HEADS)
    i = jnp.arange(BAND)[None, :]
    j = jnp.arange(2 * BAND)[:, None]
    dist = BAND + i - j
    valid = (dist >= 0) & (dist <= BAND)
    per_d = []
    for d in DILATIONS:
        b = -(slopes * (d * LOG2E))[:, None, None] * dist.astype(f32)[None]
        b = jnp.where(valid[None], b, NEG).reshape(N_PAIR, 2, 2 * BAND, BAND)
        per_d.append(b.transpose(0, 2, 1, 3).reshape(N_PAIR, 2 * BAND, 2 * BAND))
    return jnp.stack(per_d)


def _group_ones(n):
    g = jnp.arange(n) // HEAD_DIM
    return (g[:, None] == g[None, :]).astype(bf16)


def kernel(x, norm1_g, w_in, mix_conv_w, attn_out_g, conv_out_g, w_out, norm2_g, ffn_up,
           ffn_conv_w, ffn_down, final_norm_g):
    depth = w_in.shape[0]
    bias = _alibi_bias()
    gmat_c = _group_ones(D_CONV)
    gmat_a = _group_ones(LANES)
    gf = final_norm_g.reshape(1, D_MODEL)
    for layer in range(depth):
        qkv1, qkv4, qkv16, y = _mixer_in(
            x, norm1_g[layer].reshape(1, D_MODEL), w_in[layer].astype(bf16), mix_conv_w[layer],
            conv_out_g[layer].reshape(1, D_CONV), gmat_c)
        attn = _attention(qkv1, qkv4, qkv16, bias,
                          attn_out_g[layer].reshape(N_PAIR, 1, LANES), gmat_a)
        x = _out_ffn(attn, y, x, w_out[layer].astype(bf16), norm2_g[layer].reshape(1, D_MODEL),
                     ffn_up[layer].astype(bf16), ffn_conv_w[layer], ffn_down[layer].astype(bf16), gf,
                     final=(layer == depth - 1))
    return x
```

```python
import functools
import math

import jax
import jax.numpy as jnp
from jax import lax
from jax.experimental import pallas as pl
from jax.experimental.pallas import tpu as pltpu

D_MODEL = 1024
D_ATTN = 512
D_CONV = 512
HEAD_DIM = 64
N_HEADS = D_ATTN // HEAD_DIM
DILATIONS = (1, 4, 16)
BAND = 128
D_FF = 2816
EPS = 1e-6

LANES = 128
N_SLAB = 3 * D_ATTN // LANES
N_PAIR = N_HEADS // 2
FF_CHUNK = 256
N_FF_CHUNK = D_FF // FF_CHUNK
FFN_DEPTH = 3
TS = 512
TS_FFN = 512
LOG2E = 1.0 / math.log(2.0)
Q_SCALE = HEAD_DIM ** -0.5 * LOG2E
NEG = -1e30

f32 = jnp.float32
bf16 = jnp.bfloat16


def _rms_scale(x):
    return lax.rsqrt(jnp.mean(x * x, axis=-1, keepdims=True) + EPS)


def _shift_rows(x, prev8, k):
    rolled = pltpu.roll(x, k, axis=0)
    row = lax.broadcasted_iota(jnp.int32, prev8.shape, 0)
    head = jnp.where(row < k, pltpu.roll(prev8, k, axis=0), rolled[:8])
    return jnp.concatenate([head, rolled[8:]], axis=0)


def _mixer_in_kernel(x_ref, g_ref, w_ref, cw_ref, cg_ref, gmat_ref,
                     o1_ref, o4_ref, o16_ref, y_ref, nat_s, d4_s, carry_s, h_s, gb_s, cu_s):
    si = pl.program_id(1)

    @pl.when(si == 0)
    def _():
        carry_s[...] = jnp.zeros_like(carry_s)

    x = x_ref[0]
    h_s[...] = (x * _rms_scale(x) * g_ref[...]).astype(bf16)

    def project(col0):
        return jnp.dot(h_s[...], w_ref[:, col0:col0 + D_ATTN], preferred_element_type=f32)

    c0 = 3 * D_ATTN
    gb_s[...] = project(c0)
    cu_s[:8] = carry_s[...]
    cu_s[8:] = project(c0 + D_CONV) * project(c0 + 2 * D_CONV)
    carry_s[...] = cu_s[TS:]

    def conv_branch():
        cw = cw_ref[...]
        y = gb_s[...] * (cw[0:1] * cu_s[6:6 + TS] + cw[1:2] * cu_s[7:7 + TS] + cw[2:3] * cu_s[8:])
        ss = jnp.dot((y * y).astype(bf16), gmat_ref[...], preferred_element_type=f32)
        y_ref[0] = (y * lax.rsqrt(ss * (1.0 / HEAD_DIM) + EPS) * cg_ref[...]).astype(bf16)

    def qkv_part(part):
        proj = project(part * D_ATTN)
        if part == 0:
            proj = proj * Q_SCALE
        for p in range(N_PAIR):
            j = part * N_PAIR + p
            pj = proj[:, p * LANES:(p + 1) * LANES]
            nat_s[j] = pj
            o1_ref[0, j] = pj.astype(bf16)

    def relayout(part):
        for j in range(part * N_PAIR, (part + 1) * N_PAIR):
            for r in range(4):
                c4 = nat_s[j, pl.ds(r, TS // 4, stride=4), :]
                d4_s[j, r] = c4
                o4_ref[0, j, r] = c4.astype(bf16)
            for r in range(16):
                c16 = d4_s[j, r % 4, pl.ds(r // 4, TS // 16, stride=4), :]
                o16_ref[0, j, r] = c16.astype(bf16)

    qkv_part(0)
    conv_branch()
    qkv_part(1)
    relayout(0)
    qkv_part(2)
    relayout(1)
    relayout(2)


def _mixer_in(x, g, w_in, conv_w, conv_g, gmat):
    B, S, D = x.shape
    ns = S // TS
    const = dict(pipeline_mode=pl.Buffered(1))
    return pl.pallas_call(
        _mixer_in_kernel,
        grid=(B, ns),
        in_specs=[
            pl.BlockSpec((1, TS, D), lambda b, s: (b, s, 0)),
            pl.BlockSpec((1, D), lambda b, s: (0, 0), **const),
            pl.BlockSpec(w_in.shape, lambda b, s: (0, 0), **const),
            pl.BlockSpec(conv_w.shape, lambda b, s: (0, 0), **const),
            pl.BlockSpec((1, D_CONV), lambda b, s: (0, 0), **const),
            pl.BlockSpec(gmat.shape, lambda b, s: (0, 0), **const),
        ],
        out_specs=[
            pl.BlockSpec((1, N_SLAB, TS, LANES), lambda b, s: (b, 0, s, 0)),
            pl.BlockSpec((1, N_SLAB, 4, TS // 4, LANES), lambda b, s: (b, 0, 0, s, 0)),
            pl.BlockSpec((1, N_SLAB, 16, TS // 16, LANES), lambda b, s: (b, 0, 0, s, 0)),
            pl.BlockSpec((1, TS, D_CONV), lambda b, s: (b, s, 0)),
        ],
        out_shape=[
            jax.ShapeDtypeStruct((B, N_SLAB, S, LANES), bf16),
            jax.ShapeDtypeStruct((B, N_SLAB, 4, S // 4, LANES), bf16),
            jax.ShapeDtypeStruct((B, N_SLAB, 16, S // 16, LANES), bf16),
            jax.ShapeDtypeStruct((B, S, D_CONV), bf16),
        ],
        scratch_shapes=[
            pltpu.VMEM((N_SLAB, TS, LANES), f32),
            pltpu.VMEM((N_SLAB, 4, TS // 4, LANES), f32),
            pltpu.VMEM((8, D_CONV), f32),
            pltpu.VMEM((TS, D), bf16),
            pltpu.VMEM((TS, D_CONV), f32),
            pltpu.VMEM((8 + TS, D_CONV), f32),
        ],
        compiler_params=pltpu.CompilerParams(
            dimension_semantics=("arbitrary", "arbitrary"),
            vmem_limit_bytes=40 * 1024 * 1024),
        name="mixer_in",
    )(x, g, w_in, conv_w, conv_g, gmat)


ATTN_DEPTH = 4
ATTN_EMIT_DEPTH = 3


def _attn_scores(q, kcat):
    qt = q.T
    zero = jnp.zeros((HEAD_DIM, BAND), qt.dtype)
    w = jnp.concatenate([jnp.concatenate([qt[:HEAD_DIM], zero], axis=0),
                         jnp.concatenate([zero, qt[HEAD_DIM:]], axis=0)], axis=1)
    return jnp.dot(kcat, w, preferred_element_type=f32)


def _attn_values(st, vt, bias):
    s = st + bias
    m = jnp.max(s, axis=0, keepdims=True)
    p = jnp.exp2(s - m)
    l = jnp.sum(p, axis=0, keepdims=True)
    ot = jnp.dot(vt, p.astype(bf16), preferred_element_type=f32)
    return ot, 1.0 / l, m + jnp.log2(l)


def _attn_rows(ot, rl, lse):
    ot = jnp.concatenate([ot[:HEAD_DIM, :BAND] * rl[:, :BAND],
                          ot[HEAD_DIM:, BAND:] * rl[:, BAND:]], axis=0)
    lset = jnp.concatenate([jnp.broadcast_to(lse[:, :BAND], (HEAD_DIM, BAND)),
                            jnp.broadcast_to(lse[:, BAND:], (HEAD_DIM, BAND))], axis=0)
    return ot.T, lset.T


def _attn_kernel(q1_ref, k1_ref, v1_ref, q4_ref, k4_ref, v4_ref, q16_ref, k16_ref, v16_ref,
                 bias_ref, g_ref, gmat_ref, out_ref, o_s, lse_s, vt_s, st_s, ot_s, stat_s):
    S = q1_ref.shape[2]
    pair = pl.program_id(1)

    blocks = []
    for bi, (d, refs) in enumerate(zip(DILATIONS, ((q1_ref, k1_ref, v1_ref),
                                                   (q4_ref, k4_ref, v4_ref),
                                                   (q16_ref, k16_ref, v16_ref)))):
        for r in range(d):
            cls = (0, 0) if d == 1 else (0, 0, r)
            blocks += [(bi, d, refs, cls, r, n) for n in range(S // d // BAND)]

    nblk = {d: S // d // BAND for d in DILATIONS}

    def flat(blk):
        bi, d, _, _, r, n = blk
        return bi * (S // BAND) + r * nblk[d] + n

    def start(blk, t):
        _, _, (q_ref, k_ref, v_ref), cls, _, n = blk
        rows = pl.ds(n * BAND, BAND)
        vt_s[flat(blk)] = v_ref[cls + (rows,)].T
        keys = pl.ds((n - 1) * BAND, 2 * BAND) if n else rows
        st = _attn_scores(q_ref[cls + (rows,)], k_ref[cls + (keys,)])
        st_s[t % st_s.shape[0], :st.shape[0]] = st

    def finish(blk, t):
        bi, _, _, _, _, n = blk
        slot = t % st_s.shape[0]
        if n:
            vt = jnp.concatenate([vt_s[flat(blk) - 1], vt_s[flat(blk)]], axis=1)
            ot, rl, lse = _attn_values(st_s[slot], vt, bias_ref[bi, pair])
        else:
            ot, rl, lse = _attn_values(st_s[slot, :BAND], vt_s[flat(blk)],
                                       bias_ref[bi, pair, BAND:, :])
        slot = t % ot_s.shape[0]
        ot_s[slot] = ot
        stat_s[slot, 0:1] = rl
        stat_s[slot, 1:2] = lse

    def emit(blk, t):
        bi, d, _, _, r, n = blk
        slot = t % ot_s.shape[0]
        o, lse = _attn_rows(ot_s[slot], stat_s[slot, 0:1], stat_s[slot, 1:2])
        rows = pl.ds(n * (BAND * d) + r, BAND, stride=d) if d > 1 else pl.ds(n * BAND, BAND)
        o_s[bi, rows, :] = o
        lse_s[bi, rows, :] = lse

    for t in range(len(blocks) + ATTN_DEPTH + ATTN_EMIT_DEPTH):
        if t < len(blocks):
            start(blocks[t], t)
        if 0 <= t - ATTN_DEPTH < len(blocks):
            finish(blocks[t - ATTN_DEPTH], t - ATTN_DEPTH)
        if 0 <= t - ATTN_DEPTH - ATTN_EMIT_DEPTH:
            emit(blocks[t - ATTN_DEPTH - ATTN_EMIT_DEPTH], t - ATTN_DEPTH - ATTN_EMIT_DEPTH)

    def merge(j, carry):
        rows = pl.ds(pl.multiple_of(j * BAND, BAND), BAND)
        l1, l4, l16 = lse_s[0, rows, :], lse_s[1, rows, :], lse_s[2, rows, :]
        lm = jnp.maximum(jnp.maximum(l1, l4), l16)
        w1, w4, w16 = jnp.exp2(l1 - lm), jnp.exp2(l4 - lm), jnp.exp2(l16 - lm)
        num = o_s[0, rows, :] * w1 + o_s[1, rows, :] * w4 + o_s[2, rows, :] * w16
        a = num / (w1 + w4 + w16)
        ss = jnp.dot((a * a).astype(bf16), gmat_ref[...], preferred_element_type=f32)
        an = a * lax.rsqrt(ss * (1.0 / HEAD_DIM) + EPS) * g_ref[0]
        out_ref[0, 0, rows, :] = an.astype(bf16)
        return carry
    lax.fori_loop(0, S // BAND, merge, 0, unroll=True)


def _attention(qkv1, qkv4, qkv16, bias, attn_g, gmat):
    B, _, S, _ = qkv1.shape

    def slab(which):
        return (pl.BlockSpec((1, 1, S, LANES), lambda b, p: (b, which * N_PAIR + p, 0, 0)),
                pl.BlockSpec((1, 1, 4, S // 4, LANES),
                             lambda b, p: (b, which * N_PAIR + p, 0, 0, 0)),
                pl.BlockSpec((1, 1, 16, S // 16, LANES),
                             lambda b, p: (b, which * N_PAIR + p, 0, 0, 0)))

    (q1, q4, q16), (k1, k4, k16), (v1, v4, v16) = slab(0), slab(1), slab(2)
    return pl.pallas_call(
        _attn_kernel,
        grid=(B, N_PAIR),
        in_specs=[q1, k1, v1, q4, k4, v4, q16, k16, v16,
                  pl.BlockSpec(bias.shape, lambda b, p: (0, 0, 0, 0), pipeline_mode=pl.Buffered(1)),
                  pl.BlockSpec((1, 1, LANES), lambda b, p: (p, 0, 0)),
                  pl.BlockSpec(gmat.shape, lambda b, p: (0, 0), pipeline_mode=pl.Buffered(1))],
        out_specs=pl.BlockSpec((1, 1, S, LANES), lambda b, p: (b, p, 0, 0)),
        out_shape=jax.ShapeDtypeStruct((B, N_PAIR, S, LANES), bf16),
        scratch_shapes=[pltpu.VMEM((3, S, LANES), f32), pltpu.VMEM((3, S, LANES), f32),
                        pltpu.VMEM((3 * S // BAND, LANES, BAND), bf16),
                        pltpu.VMEM((ATTN_DEPTH + 1, 2 * BAND, 2 * BAND), f32),
                        pltpu.VMEM((ATTN_EMIT_DEPTH + 1, LANES, 2 * BAND), f32),
                        pltpu.VMEM((ATTN_EMIT_DEPTH + 1, 8, 2 * BAND), f32)],
        compiler_params=pltpu.CompilerParams(
            dimension_semantics=("arbitrary", "arbitrary"),
            vmem_limit_bytes=40 * 1024 * 1024),
        name="dilated_attention",
    )(qkv1, qkv1, qkv1, qkv4, qkv4, qkv4, qkv16, qkv16, qkv16, bias, attn_g, gmat)


def _out_ffn_kernel(a_ref, y_ref, x_ref, wo_ref, g2_ref, up_ref, cw_ref, down_ref, gf_ref,
                    out_ref, h2_s, acc_s, carry_s, *rings, final):
    up_slots, act_slots = rings[:FFN_DEPTH], rings[FFN_DEPTH:]
    si = pl.program_id(1)
    mix = jnp.concatenate([a_ref[0, p] for p in range(N_PAIR)] + [y_ref[0]], axis=1)
    x1 = x_ref[0] + jnp.dot(mix, wo_ref[...], preferred_element_type=f32)
    h2_s[...] = (x1 * _rms_scale(x1) * g2_ref[...]).astype(bf16)
    acc_s[...] = x1

    @pl.when(si == 0)
    def _():
        carry_s[...] = jnp.zeros_like(carry_s)

    def halves(c):
        return [pl.ds(half * D_FF + c * FF_CHUNK, FF_CHUNK) for half in range(2)]

    def up_proj(c):
        up_s = up_slots[c % FFN_DEPTH]
        up_s[:8] = carry_s[c]
        for half, cols in enumerate(halves(c)):
            up_s[8:, half * FF_CHUNK:(half + 1) * FF_CHUNK] = jnp.dot(
                h2_s[...], up_ref[:, cols], preferred_element_type=f32)
        carry_s[c] = up_s[TS_FFN:]

    def activate(c):
        up_s = up_slots[c % FFN_DEPTH]
        cw = jnp.concatenate([cw_ref[:, cols] for cols in halves(c)], axis=1)
        conv = cw[0:1] * up_s[6:6 + TS_FFN] + cw[1:2] * up_s[7:7 + TS_FFN] + cw[2:3] * up_s[8:]
        gate, val = conv[:, :FF_CHUNK], conv[:, FF_CHUNK:]
        act_slots[c % 2][...] = (gate / (1.0 + jnp.exp(-gate)) * val).astype(bf16)

    def down_proj(c):
        acc_s[...] += jnp.dot(act_slots[c % 2][...], down_ref[pl.ds(c * FF_CHUNK, FF_CHUNK), :],
                              preferred_element_type=f32)

    for t in range(N_FF_CHUNK + FFN_DEPTH):
        if t < N_FF_CHUNK:
            up_proj(t)
        if 0 <= t - (FFN_DEPTH - 1) < N_FF_CHUNK:
            activate(t - (FFN_DEPTH - 1))
        if 0 <= t - FFN_DEPTH:
            down_proj(t - FFN_DEPTH)

    out = acc_s[...]
    if final:
        out = out * _rms_scale(out) * gf_ref[...]
    out_ref[0] = out


def _out_ffn(attn, y, x, w_out, g2, up_w, conv_w, down_w, gf, final):
    B, S, D = x.shape
    ns = S // TS_FFN
    const = dict(pipeline_mode=pl.Buffered(1))
    return pl.pallas_call(
        functools.partial(_out_ffn_kernel, final=final),
        grid=(B, ns),
        in_specs=[
            pl.BlockSpec((1, N_PAIR, TS_FFN, LANES), lambda b, s: (b, 0, s, 0)),
            pl.BlockSpec((1, TS_FFN, D_CONV), lambda b, s: (b, s, 0)),
            pl.BlockSpec((1, TS_FFN, D), lambda b, s: (b, s, 0)),
            pl.BlockSpec(w_out.shape, lambda b, s: (0, 0), **const),
            pl.BlockSpec((1, D), lambda b, s: (0, 0), **const),
            pl.BlockSpec(up_w.shape, lambda b, s: (0, 0), **const),
            pl.BlockSpec(conv_w.shape, lambda b, s: (0, 0), **const),
            pl.BlockSpec(down_w.shape, lambda b, s: (0, 0), **const),
            pl.BlockSpec((1, D), lambda b, s: (0, 0), **const),
        ],
        out_specs=pl.BlockSpec((1, TS_FFN, D), lambda b, s: (b, s, 0)),
        out_shape=jax.ShapeDtypeStruct((B, S, D), f32),
        scratch_shapes=[
            pltpu.VMEM((TS_FFN, D), bf16),
            pltpu.VMEM((TS_FFN, D), f32),
            pltpu.VMEM((N_FF_CHUNK, 8, 2 * FF_CHUNK), f32),
        ] + [pltpu.VMEM((8 + TS_FFN, 2 * FF_CHUNK), f32)] * FFN_DEPTH + [
            pltpu.VMEM((TS_FFN, FF_CHUNK), bf16)] * 2 + [
        ],
        compiler_params=pltpu.CompilerParams(
            dimension_semantics=("arbitrary", "arbitrary"),
            vmem_limit_bytes=48 * 1024 * 1024),
        name="out_ffn_final" if final else "out_ffn",
    )(attn, y, x, w_out, g2, up_w, conv_w, down_w, gf)


def _alibi_bias():
    slopes = 2.0 ** (-8.0 * jnp.arange(1, N_HEADS + 1, dtype=f32) / N_HEADS)
    i = jnp.arange(BAND)[None, :]
    j = jnp.arange(2 * BAND)[:, None]
    dist = BAND + i - j
    valid = (dist >= 0) & (dist <= BAND)
    per_d = []
    for d in DILATIONS:
        b = -(slopes * (d * LOG2E))[:, None, None] * dist.astype(f32)[None]
        b = jnp.where(valid[None], b, NEG).reshape(N_PAIR, 2, 2 * BAND, BAND)
        per_d.append(b.transpose(0, 2, 1, 3).reshape(N_PAIR, 2 * BAND, 2 * BAND))
    return jnp.stack(per_d)


def _group_ones(n):
    g = jnp.arange(n) // HEAD_DIM
    return (g[:, None] == g[None, :]).astype(bf16)


def kernel(x, norm1_g, w_in, mix_conv_w, attn_out_g, conv_out_g, w_out, norm2_g, ffn_up,
           ffn_conv_w, ffn_down, final_norm_g):
    depth = w_in.shape[0]
    bias = _alibi_bias()
    gmat_c = _group_ones(D_CONV)
    gmat_a = _group_ones(LANES)
    gf = final_norm_g.reshape(1, D_MODEL)
    for layer in range(depth):
        qkv1, qkv4, qkv16, y = _mixer_in(
            x, norm1_g[layer].reshape(1, D_MODEL), w_in[layer].astype(bf16), mix_conv_w[layer],
            conv_out_g[layer].reshape(1, D_CONV), gmat_c)
        attn = _attention(qkv1, qkv4, qkv16, bias,
                          attn_out_g[layer].reshape(N_PAIR, 1, LANES), gmat_a)
        x = _out_ffn(attn, y, x, w_out[layer].astype(bf16), norm2_g[layer].reshape(1, D_MODEL),
                     ffn_up[layer].astype(bf16), ffn_conv_w[layer], ffn_down[layer].astype(bf16), gf,
                     final=(layer == depth - 1))
    return x
```

```python
import functools
import math

import jax
import jax.numpy as jnp
from jax import lax
from jax.experimental import pallas as pl
from jax.experimental.pallas import tpu as pltpu

D_MODEL = 1024
D_ATTN = 512
D_CONV = 512
HEAD_DIM = 64
N_HEADS = D_ATTN // HEAD_DIM
DILATIONS = (1, 4, 16)
BAND = 128
D_FF = 2816
EPS = 1e-6

LANES = 128
N_SLAB = 3 * D_ATTN // LANES
N_PAIR = N_HEADS // 2
FF_CHUNK = 256
N_FF_CHUNK = D_FF // FF_CHUNK
FFN_DEPTH = 3
TS = 512
TS_FFN = 512
LOG2E = 1.0 / math.log(2.0)
Q_SCALE = HEAD_DIM ** -0.5 * LOG2E
NEG = -1e30

f32 = jnp.float32
bf16 = jnp.bfloat16


def _rms_scale(x):
    return lax.rsqrt(jnp.mean(x * x, axis=-1, keepdims=True) + EPS)


def _shift_rows(x, prev8, k):
    rolled = pltpu.roll(x, k, axis=0)
    row = lax.broadcasted_iota(jnp.int32, prev8.shape, 0)
    head = jnp.where(row < k, pltpu.roll(prev8, k, axis=0), rolled[:8])
    return jnp.concatenate([head, rolled[8:]], axis=0)


def _mixer_in_kernel(x_ref, g_ref, w_ref, cw_ref, cg_ref, gmat_ref,
                     o1_ref, o4_ref, o16_ref, y_ref, nat_s, d4_s, carry_s, h_s, gb_s, cu_s):
    si = pl.program_id(1)

    @pl.when(si == 0)
    def _():
        carry_s[...] = jnp.zeros_like(carry_s)

    x = x_ref[0]
    h_s[...] = (x * _rms_scale(x) * g_ref[...]).astype(bf16)

    def project(col0):
        return jnp.dot(h_s[...], w_ref[:, col0:col0 + D_ATTN], preferred_element_type=f32)

    c0 = 3 * D_ATTN
    gb_s[...] = project(c0)
    cu_s[:8] = carry_s[...]
    cu_s[8:] = project(c0 + D_CONV) * project(c0 + 2 * D_CONV)
    carry_s[...] = cu_s[TS:]

    def conv_branch():
        cw = cw_ref[...]
        y = gb_s[...] * (cw[0:1] * cu_s[6:6 + TS] + cw[1:2] * cu_s[7:7 + TS] + cw[2:3] * cu_s[8:])
        ss = jnp.dot((y * y).astype(bf16), gmat_ref[...], preferred_element_type=f32)
        y_ref[0] = (y * lax.rsqrt(ss * (1.0 / HEAD_DIM) + EPS) * cg_ref[...]).astype(bf16)

    def qkv_part(part):
        proj = project(part * D_ATTN)
        if part == 0:
            proj = proj * Q_SCALE
        for p in range(N_PAIR):
            j = part * N_PAIR + p
            pj = proj[:, p * LANES:(p + 1) * LANES]
            nat_s[j] = pj
            o1_ref[0, j] = pj.astype(bf16)

    def relayout(part):
        for j in range(part * N_PAIR, (part + 1) * N_PAIR):
            for r in range(4):
                c4 = nat_s[j, pl.ds(r, TS // 4, stride=4), :]
                d4_s[j, r] = c4
                o4_ref[0, j, r] = c4.astype(bf16)
            for r in range(16):
                c16 = d4_s[j, r % 4, pl.ds(r // 4, TS // 16, stride=4), :]
                o16_ref[0, j, r] = c16.astype(bf16)

    qkv_part(0)
    conv_branch()
    qkv_part(1)
    relayout(0)
    qkv_part(2)
    relayout(1)
    relayout(2)


def _mixer_in(x, g, w_in, conv_w, conv_g, gmat):
    B, S, D = x.shape
    ns = S // TS
    const = dict(pipeline_mode=pl.Buffered(1))
    return pl.pallas_call(
        _mixer_in_kernel,
        grid=(B, ns),
        in_specs=[
            pl.BlockSpec((1, TS, D), lambda b, s: (b, s, 0)),
            pl.BlockSpec((1, D), lambda b, s: (0, 0), **const),
            pl.BlockSpec(w_in.shape, lambda b, s: (0, 0), **const),
            pl.BlockSpec(conv_w.shape, lambda b, s: (0, 0), **const),
            pl.BlockSpec((1, D_CONV), lambda b, s: (0, 0), **const),
            pl.BlockSpec(gmat.shape, lambda b, s: (0, 0), **const),
        ],
        out_specs=[
            pl.BlockSpec((1, N_SLAB, TS, LANES), lambda b, s: (b, 0, s, 0)),
            pl.BlockSpec((1, N_SLAB, 4, TS // 4, LANES), lambda b, s: (b, 0, 0, s, 0)),
            pl.BlockSpec((1, N_SLAB, 16, TS // 16, LANES), lambda b, s: (b, 0, 0, s, 0)),
            pl.BlockSpec((1, TS, D_CONV), lambda b, s: (b, s, 0)),
        ],
        out_shape=[
            jax.ShapeDtypeStruct((B, N_SLAB, S, LANES), bf16),
            jax.ShapeDtypeStruct((B, N_SLAB, 4, S // 4, LANES), bf16),
            jax.ShapeDtypeStruct((B, N_SLAB, 16, S // 16, LANES), bf16),
            jax.ShapeDtypeStruct((B, S, D_CONV), bf16),
        ],
        scratch_shapes=[
            pltpu.VMEM((N_SLAB, TS, LANES), f32),
            pltpu.VMEM((N_SLAB, 4, TS // 4, LANES), f32),
            pltpu.VMEM((8, D_CONV), f32),
            pltpu.VMEM((TS, D), bf16),
            pltpu.VMEM((TS, D_CONV), f32),
            pltpu.VMEM((8 + TS, D_CONV), f32),
        ],
        compiler_params=pltpu.CompilerParams(
            dimension_semantics=("arbitrary", "arbitrary"),
            vmem_limit_bytes=40 * 1024 * 1024),
        name="mixer_in",
    )(x, g, w_in, conv_w, conv_g, gmat)


ATTN_DEPTH = 4
ATTN_EMIT_DEPTH = 3


def _attn_scores(q, kcat):
    qt = q.T
    zero = jnp.zeros((HEAD_DIM, BAND), qt.dtype)
    w = jnp.concatenate([jnp.concatenate([qt[:HEAD_DIM], zero], axis=0),
                         jnp.concatenate([zero, qt[HEAD_DIM:]], axis=0)], axis=1)
    return jnp.dot(kcat, w, preferred_element_type=f32)


def _attn_values(st, vt, bias):
    s = st + bias
    m = jnp.max(s, axis=0, keepdims=True)
    p = jnp.exp2(s - m)
    l = jnp.sum(p, axis=0, keepdims=True)
    ot = jnp.dot(vt, p.astype(bf16), preferred_element_type=f32)
    return ot, 1.0 / l, m + jnp.log2(l)


def _attn_rows(ot, rl, lse):
    ot = jnp.concatenate([ot[:HEAD_DIM, :BAND] * rl[:, :BAND],
                          ot[HEAD_DIM:, BAND:] * rl[:, BAND:]], axis=0)
    lset = jnp.concatenate([jnp.broadcast_to(lse[:, :BAND], (HEAD_DIM, BAND)),
                            jnp.broadcast_to(lse[:, BAND:], (HEAD_DIM, BAND))], axis=0)
    return ot.T, lset.T


def _attn_kernel(q1_ref, k1_ref, v1_ref, q4_ref, k4_ref, v4_ref, q16_ref, k16_ref, v16_ref,
                 bias_ref, g_ref, gmat_ref, out_ref, o_s, lse_s, vt_s, st_s, ot_s, stat_s):
    S = q1_ref.shape[2]
    pair = pl.program_id(1)

    per_branch = []
    for bi, (d, refs) in enumerate(zip(DILATIONS, ((q1_ref, k1_ref, v1_ref),
                                                   (q4_ref, k4_ref, v4_ref),
                                                   (q16_ref, k16_ref, v16_ref)))):
        per_branch.append([(bi, d, refs, (0, 0) if d == 1 else (0, 0, r), r, n)
                           for n in range(S // d // BAND) for r in range(d)])
    blocks = [blk for pair in zip(per_branch[2], per_branch[1]) for blk in pair] + per_branch[0]

    nblk = {d: S // d // BAND for d in DILATIONS}

    def flat(blk):
        bi, d, _, _, r, n = blk
        return bi * (S // BAND) + r * nblk[d] + n

    def start(blk, t):
        _, _, (q_ref, k_ref, v_ref), cls, _, n = blk
        rows = pl.ds(n * BAND, BAND)
        vt_s[flat(blk)] = v_ref[cls + (rows,)].T
        keys = pl.ds((n - 1) * BAND, 2 * BAND) if n else rows
        st = _attn_scores(q_ref[cls + (rows,)], k_ref[cls + (keys,)])
        st_s[t % st_s.shape[0], :st.shape[0]] = st

    def finish(blk, t):
        bi, _, _, _, _, n = blk
        slot = t % st_s.shape[0]
        if n:
            vt = jnp.concatenate([vt_s[flat(blk) - 1], vt_s[flat(blk)]], axis=1)
            ot, rl, lse = _attn_values(st_s[slot], vt, bias_ref[bi, pair])
        else:
            ot, rl, lse = _attn_values(st_s[slot, :BAND], vt_s[flat(blk)],
                                       bias_ref[bi, pair, BAND:, :])
        slot = t % ot_s.shape[0]
        ot_s[slot] = ot
        stat_s[slot, 0:1] = rl
        stat_s[slot, 1:2] = lse

    def emit(blk, t):
        bi, d, _, _, r, n = blk
        slot = t % ot_s.shape[0]
        o, lse = _attn_rows(ot_s[slot], stat_s[slot, 0:1], stat_s[slot, 1:2])
        rows = pl.ds(n * (BAND * d) + r, BAND, stride=d) if d > 1 else pl.ds(n * BAND, BAND)
        o_s[bi, rows, :] = o
        lse_s[bi, rows, :] = lse

    def merge(j):
        rows = pl.ds(j * BAND, BAND)
        l1, l4, l16 = lse_s[0, rows, :], lse_s[1, rows, :], lse_s[2, rows, :]
        lm = jnp.maximum(jnp.maximum(l1, l4), l16)
        w1, w4, w16 = jnp.exp2(l1 - lm), jnp.exp2(l4 - lm), jnp.exp2(l16 - lm)
        num = o_s[0, rows, :] * w1 + o_s[1, rows, :] * w4 + o_s[2, rows, :] * w16
        a = num / (w1 + w4 + w16)
        ss = jnp.dot((a * a).astype(bf16), gmat_ref[...], preferred_element_type=f32)
        an = a * lax.rsqrt(ss * (1.0 / HEAD_DIM) + EPS) * g_ref[0]
        out_ref[0, 0, rows, :] = an.astype(bf16)

    for t in range(len(blocks) + ATTN_DEPTH + ATTN_EMIT_DEPTH):
        if t < len(blocks):
            start(blocks[t], t)
        if 0 <= t - ATTN_DEPTH < len(blocks):
            finish(blocks[t - ATTN_DEPTH], t - ATTN_DEPTH)
        done = t - ATTN_DEPTH - ATTN_EMIT_DEPTH
        if 0 <= done:
            emit(blocks[done], done)
            if blocks[done][0] == 0:
                merge(blocks[done][5])


def _attention(qkv1, qkv4, qkv16, bias, attn_g, gmat):
    B, _, S, _ = qkv1.shape

    def slab(which):
        return (pl.BlockSpec((1, 1, S, LANES), lambda b, p: (b, which * N_PAIR + p, 0, 0)),
                pl.BlockSpec((1, 1, 4, S // 4, LANES),
                             lambda b, p: (b, which * N_PAIR + p, 0, 0, 0)),
                pl.BlockSpec((1, 1, 16, S // 16, LANES),
                             lambda b, p: (b, which * N_PAIR + p, 0, 0, 0)))

    (q1, q4, q16), (k1, k4, k16), (v1, v4, v16) = slab(0), slab(1), slab(2)
    return pl.pallas_call(
        _attn_kernel,
        grid=(B, N_PAIR),
        in_specs=[q1, k1, v1, q4, k4, v4, q16, k16, v16,
                  pl.BlockSpec(bias.shape, lambda b, p: (0, 0, 0, 0), pipeline_mode=pl.Buffered(1)),
                  pl.BlockSpec((1, 1, LANES), lambda b, p: (p, 0, 0)),
                  pl.BlockSpec(gmat.shape, lambda b, p: (0, 0), pipeline_mode=pl.Buffered(1))],
        out_specs=pl.BlockSpec((1, 1, S, LANES), lambda b, p: (b, p, 0, 0)),
        out_shape=jax.ShapeDtypeStruct((B, N_PAIR, S, LANES), bf16),
        scratch_shapes=[pltpu.VMEM((3, S, LANES), f32), pltpu.VMEM((3, S, LANES), f32),
                        pltpu.VMEM((3 * S // BAND, LANES, BAND), bf16),
                        pltpu.VMEM((ATTN_DEPTH + 1, 2 * BAND, 2 * BAND), f32),
                        pltpu.VMEM((ATTN_EMIT_DEPTH + 1, LANES, 2 * BAND), f32),
                        pltpu.VMEM((ATTN_EMIT_DEPTH + 1, 8, 2 * BAND), f32)],
        compiler_params=pltpu.CompilerParams(
            dimension_semantics=("arbitrary", "arbitrary"),
            vmem_limit_bytes=40 * 1024 * 1024),
        name="dilated_attention",
    )(qkv1, qkv1, qkv1, qkv4, qkv4, qkv4, qkv16, qkv16, qkv16, bias, attn_g, gmat)


def _out_ffn_kernel(a_ref, y_ref, x_ref, wo_ref, g2_ref, up_ref, cw_ref, down_ref, gf_ref,
                    out_ref, h2_s, acc_s, carry_s, *rings, final):
    up_slots, act_slots = rings[:FFN_DEPTH], rings[FFN_DEPTH:]
    si = pl.program_id(1)
    mix = jnp.concatenate([a_ref[0, p] for p in range(N_PAIR)] + [y_ref[0]], axis=1)
    x1 = x_ref[0] + jnp.dot(mix, wo_ref[...], preferred_element_type=f32)
    h2_s[...] = (x1 * _rms_scale(x1) * g2_ref[...]).astype(bf16)
    acc_s[...] = x1

    @pl.when(si == 0)
    def _():
        carry_s[...] = jnp.zeros_like(carry_s)

    def halves(c):
        return [pl.ds(half * D_FF + c * FF_CHUNK, FF_CHUNK) for half in range(2)]

    def up_proj(c):
        up_s = up_slots[c % FFN_DEPTH]
        up_s[:8] = carry_s[c]
        for half, cols in enumerate(halves(c)):
            up_s[8:, half * FF_CHUNK:(half + 1) * FF_CHUNK] = jnp.dot(
                h2_s[...], up_ref[:, cols], preferred_element_type=f32)
        carry_s[c] = up_s[TS_FFN:]

    def activate(c):
        up_s = up_slots[c % FFN_DEPTH]
        cw = jnp.concatenate([cw_ref[:, cols] for cols in halves(c)], axis=1)
        conv = cw[0:1] * up_s[6:6 + TS_FFN] + cw[1:2] * up_s[7:7 + TS_FFN] + cw[2:3] * up_s[8:]
        gate, val = conv[:, :FF_CHUNK], conv[:, FF_CHUNK:]
        act_slots[c % 2][...] = (gate / (1.0 + jnp.exp(-gate)) * val).astype(bf16)

    def down_proj(c):
        acc_s[...] += jnp.dot(act_slots[c % 2][...], down_ref[pl.ds(c * FF_CHUNK, FF_CHUNK), :],
                              preferred_element_type=f32)

    for t in range(N_FF_CHUNK + FFN_DEPTH):
        if t < N_FF_CHUNK:
            up_proj(t)
        if 0 <= t - (FFN_DEPTH - 1) < N_FF_CHUNK:
            activate(t - (FFN_DEPTH - 1))
        if 0 <= t - FFN_DEPTH:
            down_proj(t - FFN_DEPTH)

    out = acc_s[...]
    if final:
        out = out * _rms_scale(out) * gf_ref[...]
    out_ref[0] = out


def _out_ffn(attn, y, x, w_out, g2, up_w, conv_w, down_w, gf, final):
    B, S, D = x.shape
    ns = S // TS_FFN
    const = dict(pipeline_mode=pl.Buffered(1))
    return pl.pallas_call(
        functools.partial(_out_ffn_kernel, final=final),
        grid=(B, ns),
        in_specs=[
            pl.BlockSpec((1, N_PAIR, TS_FFN, LANES), lambda b, s: (b, 0, s, 0)),
            pl.BlockSpec((1, TS_FFN, D_CONV), lambda b, s: (b, s, 0)),
            pl.BlockSpec((1, TS_FFN, D), lambda b, s: (b, s, 0)),
            pl.BlockSpec(w_out.shape, lambda b, s: (0, 0), **const),
            pl.BlockSpec((1, D), lambda b, s: (0, 0), **const),
            pl.BlockSpec(up_w.shape, lambda b, s: (0, 0), **const),
            pl.BlockSpec(conv_w.shape, lambda b, s: (0, 0), **const),
            pl.BlockSpec(down_w.shape, lambda b, s: (0, 0), **const),
            pl.BlockSpec((1, D), lambda b, s: (0, 0), **const),
        ],
        out_specs=pl.BlockSpec((1, TS_FFN, D), lambda b, s: (b, s, 0)),
        out_shape=jax.ShapeDtypeStruct((B, S, D), f32),
        scratch_shapes=[
            pltpu.VMEM((TS_FFN, D), bf16),
            pltpu.VMEM((TS_FFN, D), f32),
            pltpu.VMEM((N_FF_CHUNK, 8, 2 * FF_CHUNK), f32),
        ] + [pltpu.VMEM((8 + TS_FFN, 2 * FF_CHUNK), f32)] * FFN_DEPTH + [
            pltpu.VMEM((TS_FFN, FF_CHUNK), bf16)] * 2 + [
        ],
        compiler_params=pltpu.CompilerParams(
            dimension_semantics=("arbitrary", "arbitrary"),
            vmem_limit_bytes=48 * 1024 * 1024),
        name="out_ffn_final" if final else "out_ffn",
    )(attn, y, x, w_out, g2, up_w, conv_w, down_w, gf)


def _alibi_bias():
    slopes = 2.0 ** (-8.0 * jnp.arange(1, N_HEADS + 1, dtype=f32) / N_HEADS)
    i = jnp.arange(BAND)[None, :]
    j = jnp.arange(2 * BAND)[:, None]
    dist = BAND + i - j
    valid = (dist >= 0) & (dist <= BAND)
    per_d = []
    for d in DILATIONS:
        b = -(slopes * (d * LOG2E))[:, None, None] * dist.astype(f32)[None]
        b = jnp.where(valid[None], b, NEG).reshape(N_PAIR, 2, 2 * BAND, BAND)
        per_d.append(b.transpose(0, 2, 1, 3).reshape(N_PAIR, 2 * BAND, 2 * BAND))
    return jnp.stack(per_d)


def _group_ones(n):
    g = jnp.arange(n) // HEAD_DIM
    return (g[:, None] == g[None, :]).astype(bf16)


def kernel(x, norm1_g, w_in, mix_conv_w, attn_out_g, conv_out_g, w_out, norm2_g, ffn_up,
           ffn_conv_w, ffn_down, final_norm_g):
    depth = w_in.shape[0]
    bias = _alibi_bias()
    gmat_c = _group_ones(D_CONV)
    gmat_a = _group_ones(LANES)
    gf = final_norm_g.reshape(1, D_MODEL)
    for layer in range(depth):
        qkv1, qkv4, qkv16, y = _mixer_in(
            x, norm1_g[layer].reshape(1, D_MODEL), w_in[layer].astype(bf16), mix_conv_w[layer],
            conv_out_g[layer].reshape(1, D_CONV), gmat_c)
        attn = _attention(qkv1, qkv4, qkv16, bias,
                          attn_out_g[layer].reshape(N_PAIR, 1, LANES), gmat_a)
        x = _out_ffn(attn, y, x, w_out[layer].astype(bf16), norm2_g[layer].reshape(1, D_MODEL),
                     ffn_up[layer].astype(bf16), ffn_conv_w[layer], ffn_down[layer].astype(bf16), gf,
                     final=(layer == depth - 1))
    return x
```

```python
import functools
import math

import jax
import jax.numpy as jnp
from jax import lax
from jax.experimental import pallas as pl
from jax.experimental.pallas import tpu as pltpu

D_MODEL = 1024
D_ATTN = 512
D_CONV = 512
HEAD_DIM = 64
N_HEADS = D_ATTN // HEAD_DIM
DILATIONS = (1, 4, 16)
BAND = 128
D_FF = 2816
EPS = 1e-6

LANES = 128
N_SLAB = 3 * D_ATTN // LANES
N_PAIR = N_HEADS // 2
FF_CHUNK = 256
N_FF_CHUNK = D_FF // FF_CHUNK
FFN_DEPTH = 3
FFN_DOWN_GROUP = 2
TS = 512
TS_FFN = 512
LOG2E = 1.0 / math.log(2.0)
Q_SCALE = HEAD_DIM ** -0.5 * LOG2E
NEG = -1e30

f32 = jnp.float32
bf16 = jnp.bfloat16


def _rms_scale(x):
    return lax.rsqrt(jnp.mean(x * x, axis=-1, keepdims=True) + EPS)


def _shift_rows(x, prev8, k):
    rolled = pltpu.roll(x, k, axis=0)
    row = lax.broadcasted_iota(jnp.int32, prev8.shape, 0)
    head = jnp.where(row < k, pltpu.roll(prev8, k, axis=0), rolled[:8])
    return jnp.concatenate([head, rolled[8:]], axis=0)


def _mixer_in_kernel(x_ref, g_ref, w_ref, cw_ref, cg_ref, gmat_ref,
                     o1_ref, o4_ref, o16_ref, y_ref, nat_s, d4_s, carry_s, h_s, gb_s, cu_s):
    si = pl.program_id(1)

    @pl.when(si == 0)
    def _():
        carry_s[...] = jnp.zeros_like(carry_s)

    x = x_ref[0]
    h_s[...] = (x * _rms_scale(x) * g_ref[...]).astype(bf16)

    def project(col0):
        return jnp.dot(h_s[...], w_ref[:, col0:col0 + D_ATTN], preferred_element_type=f32)

    c0 = 3 * D_ATTN
    gb_s[...] = project(c0)
    cu_s[:8] = carry_s[...]
    cu_s[8:] = project(c0 + D_CONV) * project(c0 + 2 * D_CONV)
    carry_s[...] = cu_s[TS:]

    def conv_branch():
        cw = cw_ref[...]
        y = gb_s[...] * (cw[0:1] * cu_s[6:6 + TS] + cw[1:2] * cu_s[7:7 + TS] + cw[2:3] * cu_s[8:])
        ss = jnp.dot((y * y).astype(bf16), gmat_ref[...], preferred_element_type=f32)
        y_ref[0] = (y * lax.rsqrt(ss * (1.0 / HEAD_DIM) + EPS) * cg_ref[...]).astype(bf16)

    def qkv_part(part):
        proj = project(part * D_ATTN)
        if part == 0:
            proj = proj * Q_SCALE
        for p in range(N_PAIR):
            j = part * N_PAIR + p
            pj = proj[:, p * LANES:(p + 1) * LANES]
            nat_s[j] = pj
            o1_ref[0, j] = pj.astype(bf16)

    def relayout(part):
        for j in range(part * N_PAIR, (part + 1) * N_PAIR):
            for r in range(4):
                c4 = nat_s[j, pl.ds(r, TS // 4, stride=4), :]
                d4_s[j, r] = c4
                o4_ref[0, j, r] = c4.astype(bf16)
            for r in range(16):
                c16 = d4_s[j, r % 4, pl.ds(r // 4, TS // 16, stride=4), :]
                o16_ref[0, j, r] = c16.astype(bf16)

    qkv_part(0)
    conv_branch()
    qkv_part(1)
    relayout(0)
    qkv_part(2)
    relayout(1)
    relayout(2)


def _mixer_in(x, g, w_in, conv_w, conv_g, gmat, layer):
    B, S, D = x.shape
    ns = S // TS
    const = dict(pipeline_mode=pl.Buffered(1))
    return pl.pallas_call(
        _mixer_in_kernel,
        grid=(B, ns),
        in_specs=[
            pl.BlockSpec((1, TS, D), lambda b, s: (b, s, 0)),
            pl.BlockSpec((1, D), lambda b, s: (0, 0), **const),
            pl.BlockSpec((None,) + w_in.shape[1:], lambda b, s: (layer, 0, 0), **const),
            pl.BlockSpec(conv_w.shape, lambda b, s: (0, 0), **const),
            pl.BlockSpec((1, D_CONV), lambda b, s: (0, 0), **const),
            pl.BlockSpec(gmat.shape, lambda b, s: (0, 0), **const),
        ],
        out_specs=[
            pl.BlockSpec((1, N_SLAB, TS, LANES), lambda b, s: (b, 0, s, 0)),
            pl.BlockSpec((1, N_SLAB, 4, TS // 4, LANES), lambda b, s: (b, 0, 0, s, 0)),
            pl.BlockSpec((1, N_SLAB, 16, TS // 16, LANES), lambda b, s: (b, 0, 0, s, 0)),
            pl.BlockSpec((1, TS, D_CONV), lambda b, s: (b, s, 0)),
        ],
        out_shape=[
            jax.ShapeDtypeStruct((B, N_SLAB, S, LANES), bf16),
            jax.ShapeDtypeStruct((B, N_SLAB, 4, S // 4, LANES), bf16),
            jax.ShapeDtypeStruct((B, N_SLAB, 16, S // 16, LANES), bf16),
            jax.ShapeDtypeStruct((B, S, D_CONV), bf16),
        ],
        scratch_shapes=[
            pltpu.VMEM((N_SLAB, TS, LANES), f32),
            pltpu.VMEM((N_SLAB, 4, TS // 4, LANES), f32),
            pltpu.VMEM((8, D_CONV), f32),
            pltpu.VMEM((TS, D), bf16),
            pltpu.VMEM((TS, D_CONV), f32),
            pltpu.VMEM((8 + TS, D_CONV), f32),
        ],
        compiler_params=pltpu.CompilerParams(
            dimension_semantics=("arbitrary", "arbitrary"),
            vmem_limit_bytes=40 * 1024 * 1024),
        name="mixer_in",
    )(x, g, w_in, conv_w, conv_g, gmat)


ATTN_DEPTH = 4
ATTN_EMIT_DEPTH = 3


def _attn_scores(q, kcat):
    qt = q.T
    zero = jnp.zeros((HEAD_DIM, BAND), qt.dtype)
    w = jnp.concatenate([jnp.concatenate([qt[:HEAD_DIM], zero], axis=0),
                         jnp.concatenate([zero, qt[HEAD_DIM:]], axis=0)], axis=1)
    return jnp.dot(kcat, w, preferred_element_type=f32)


def _attn_values(st, vt, bias):
    s = st + bias
    m = jnp.max(s, axis=0, keepdims=True)
    p = jnp.exp2(s - m)
    l = jnp.sum(p, axis=0, keepdims=True)
    ot = jnp.dot(vt, p.astype(bf16), preferred_element_type=f32)
    return ot, 1.0 / l, m + jnp.log2(l)


def _attn_rows(ot, rl, lse):
    ot = jnp.concatenate([ot[:HEAD_DIM, :BAND] * rl[:, :BAND],
                          ot[HEAD_DIM:, BAND:] * rl[:, BAND:]], axis=0)
    lset = jnp.concatenate([jnp.broadcast_to(lse[:, :BAND], (HEAD_DIM, BAND)),
                            jnp.broadcast_to(lse[:, BAND:], (HEAD_DIM, BAND))], axis=0)
    return ot.T, lset.T


def _attn_kernel(q1_ref, k1_ref, v1_ref, q4_ref, k4_ref, v4_ref, q16_ref, k16_ref, v16_ref,
                 bias_ref, g_ref, gmat_ref, out_ref, o_s, lse_s, vt_s, st_s, ot_s, stat_s):
    S = q1_ref.shape[2]
    pair = pl.program_id(1)

    per_branch = []
    for bi, (d, refs) in enumerate(zip(DILATIONS, ((q1_ref, k1_ref, v1_ref),
                                                   (q4_ref, k4_ref, v4_ref),
                                                   (q16_ref, k16_ref, v16_ref)))):
        per_branch.append([(bi, d, refs, (0, 0) if d == 1 else (0, 0, r), r, n)
                           for n in range(S // d // BAND) for r in range(d)])
    blocks = [blk for pair in zip(per_branch[2], per_branch[1]) for blk in pair] + per_branch[0]

    nblk = {d: S // d // BAND for d in DILATIONS}

    def flat(blk):
        bi, d, _, _, r, n = blk
        return bi * (S // BAND) + r * nblk[d] + n

    def start(blk, t):
        _, _, (q_ref, k_ref, v_ref), cls, _, n = blk
        rows = pl.ds(n * BAND, BAND)
        vt_s[flat(blk)] = v_ref[cls + (rows,)].T
        keys = pl.ds((n - 1) * BAND, 2 * BAND) if n else rows
        st = _attn_scores(q_ref[cls + (rows,)], k_ref[cls + (keys,)])
        st_s[t % st_s.shape[0], :st.shape[0]] = st

    def finish(blk, t):
        bi, _, _, _, _, n = blk
        slot = t % st_s.shape[0]
        if n:
            vt = jnp.concatenate([vt_s[flat(blk) - 1], vt_s[flat(blk)]], axis=1)
            ot, rl, lse = _attn_values(st_s[slot], vt, bias_ref[bi, pair])
        else:
            ot, rl, lse = _attn_values(st_s[slot, :BAND], vt_s[flat(blk)],
                                       bias_ref[bi, pair, BAND:, :])
        slot = t % ot_s.shape[0]
        ot_s[slot] = ot
        stat_s[slot, 0:1] = rl
        stat_s[slot, 1:2] = lse

    def emit(blk, t):
        bi, d, _, _, r, n = blk
        slot = t % ot_s.shape[0]
        o, lse = _attn_rows(ot_s[slot], stat_s[slot, 0:1], stat_s[slot, 1:2])
        rows = pl.ds(n * (BAND * d) + r, BAND, stride=d) if d > 1 else pl.ds(n * BAND, BAND)
        o_s[bi, rows, :] = o
        lse_s[bi, rows, :] = lse

    def merge(j):
        rows = pl.ds(j * BAND, BAND)
        l1, l4, l16 = lse_s[0, rows, :], lse_s[1, rows, :], lse_s[2, rows, :]
        lm = jnp.maximum(jnp.maximum(l1, l4), l16)
        w1, w4, w16 = jnp.exp2(l1 - lm), jnp.exp2(l4 - lm), jnp.exp2(l16 - lm)
        num = o_s[0, rows, :] * w1 + o_s[1, rows, :] * w4 + o_s[2, rows, :] * w16
        a = num / (w1 + w4 + w16)
        ss = jnp.dot((a * a).astype(bf16), gmat_ref[...], preferred_element_type=f32)
        an = a * lax.rsqrt(ss * (1.0 / HEAD_DIM) + EPS) * g_ref[0]
        out_ref[0, 0, rows, :] = an.astype(bf16)

    for t in range(len(blocks) + ATTN_DEPTH + ATTN_EMIT_DEPTH):
        if t < len(blocks):
            start(blocks[t], t)
        if 0 <= t - ATTN_DEPTH < len(blocks):
            finish(blocks[t - ATTN_DEPTH], t - ATTN_DEPTH)
        done = t - ATTN_DEPTH - ATTN_EMIT_DEPTH
        if 0 <= done:
            emit(blocks[done], done)
            if blocks[done][0] == 0:
                merge(blocks[done][5])


def _attention(qkv1, qkv4, qkv16, bias, attn_g, gmat):
    B, _, S, _ = qkv1.shape

    def slab(which):
        return (pl.BlockSpec((1, 1, S, LANES), lambda b, p: (b, which * N_PAIR + p, 0, 0)),
                pl.BlockSpec((1, 1, 4, S // 4, LANES),
                             lambda b, p: (b, which * N_PAIR + p, 0, 0, 0)),
                pl.BlockSpec((1, 1, 16, S // 16, LANES),
                             lambda b, p: (b, which * N_PAIR + p, 0, 0, 0)))

    (q1, q4, q16), (k1, k4, k16), (v1, v4, v16) = slab(0), slab(1), slab(2)
    return pl.pallas_call(
        _attn_kernel,
        grid=(B, N_PAIR),
        in_specs=[q1, k1, v1, q4, k4, v4, q16, k16, v16,
                  pl.BlockSpec(bias.shape, lambda b, p: (0, 0, 0, 0), pipeline_mode=pl.Buffered(1)),
                  pl.BlockSpec((1, 1, LANES), lambda b, p: (p, 0, 0)),
                  pl.BlockSpec(gmat.shape, lambda b, p: (0, 0), pipeline_mode=pl.Buffered(1))],
        out_specs=pl.BlockSpec((1, 1, S, LANES), lambda b, p: (b, p, 0, 0)),
        out_shape=jax.ShapeDtypeStruct((B, N_PAIR, S, LANES), bf16),
        scratch_shapes=[pltpu.VMEM((3, S, LANES), f32), pltpu.VMEM((3, S, LANES), f32),
                        pltpu.VMEM((3 * S // BAND, LANES, BAND), bf16),
                        pltpu.VMEM((ATTN_DEPTH + 1, 2 * BAND, 2 * BAND), f32),
                        pltpu.VMEM((ATTN_EMIT_DEPTH + 1, LANES, 2 * BAND), f32),
                        pltpu.VMEM((ATTN_EMIT_DEPTH + 1, 8, 2 * BAND), f32)],
        compiler_params=pltpu.CompilerParams(
            dimension_semantics=("arbitrary", "arbitrary"),
            vmem_limit_bytes=40 * 1024 * 1024),
        name="dilated_attention",
    )(qkv1, qkv1, qkv1, qkv4, qkv4, qkv4, qkv16, qkv16, qkv16, bias, attn_g, gmat)


def _out_ffn_kernel(a_ref, y_ref, x_ref, wo_ref, g2_ref, up_ref, cw_ref, down_ref, gf_ref,
                    out_ref, h2_s, acc_s, carry_s, *rings, final):
    up_slots, act_slots = rings[:FFN_DEPTH], rings[FFN_DEPTH:]
    si = pl.program_id(1)
    mix = jnp.concatenate([a_ref[0, p] for p in range(N_PAIR)] + [y_ref[0]], axis=1)
    x1 = x_ref[0] + jnp.dot(mix, wo_ref[...], preferred_element_type=f32)
    h2_s[...] = (x1 * _rms_scale(x1) * g2_ref[...]).astype(bf16)
    acc_s[...] = x1

    @pl.when(si == 0)
    def _():
        carry_s[...] = jnp.zeros_like(carry_s)

    def halves(c):
        return [pl.ds(half * D_FF + c * FF_CHUNK, FF_CHUNK) for half in range(2)]

    def up_proj(c):
        up_s = up_slots[c % FFN_DEPTH]
        up_s[:8] = carry_s[c]
        for half, cols in enumerate(halves(c)):
            up_s[8:, half * FF_CHUNK:(half + 1) * FF_CHUNK] = jnp.dot(
                h2_s[...], up_ref[:, cols], preferred_element_type=f32)
        carry_s[c] = up_s[TS_FFN:]

    def activate(c):
        up_s = up_slots[c % FFN_DEPTH]
        cw = jnp.concatenate([cw_ref[:, cols] for cols in halves(c)], axis=1)
        conv = cw[0:1] * up_s[6:6 + TS_FFN] + cw[1:2] * up_s[7:7 + TS_FFN] + cw[2:3] * up_s[8:]
        gate, val = conv[:, :FF_CHUNK], conv[:, FF_CHUNK:]
        act = (gate / (1.0 + jnp.exp(-gate)) * val).astype(bf16)
        g = FFN_DOWN_GROUP
        act_slots[c // g % 2][:, c % g * FF_CHUNK:(c % g + 1) * FF_CHUNK] = act

    def down_proj(c):
        first = c - c % FFN_DOWN_GROUP
        width = (c - first + 1) * FF_CHUNK
        acc_s[...] += jnp.dot(act_slots[c // FFN_DOWN_GROUP % 2][:, :width],
                              down_ref[pl.ds(first * FF_CHUNK, width), :],
                              preferred_element_type=f32)

    for t in range(N_FF_CHUNK + FFN_DEPTH):
        if t < N_FF_CHUNK:
            up_proj(t)
        if 0 <= t - (FFN_DEPTH - 1) < N_FF_CHUNK:
            activate(t - (FFN_DEPTH - 1))
        c = t - FFN_DEPTH
        if 0 <= c and ((c + 1) % FFN_DOWN_GROUP == 0 or c == N_FF_CHUNK - 1):
            down_proj(c)

    out = acc_s[...]
    if final:
        out = out * _rms_scale(out) * gf_ref[...]
    out_ref[0] = out


def _out_ffn(attn, y, x, w_out, g2, up_w, conv_w, down_w, gf, layer, final):
    B, S, D = x.shape
    ns = S // TS_FFN
    const = dict(pipeline_mode=pl.Buffered(1))
    return pl.pallas_call(
        functools.partial(_out_ffn_kernel, final=final),
        grid=(B, ns),
        in_specs=[
            pl.BlockSpec((1, N_PAIR, TS_FFN, LANES), lambda b, s: (b, 0, s, 0)),
            pl.BlockSpec((1, TS_FFN, D_CONV), lambda b, s: (b, s, 0)),
            pl.BlockSpec((1, TS_FFN, D), lambda b, s: (b, s, 0)),
            pl.BlockSpec((None,) + w_out.shape[1:], lambda b, s: (layer, 0, 0), **const),
            pl.BlockSpec((1, D), lambda b, s: (0, 0), **const),
            pl.BlockSpec((None,) + up_w.shape[1:], lambda b, s: (layer, 0, 0), **const),
            pl.BlockSpec(conv_w.shape, lambda b, s: (0, 0), **const),
            pl.BlockSpec((None,) + down_w.shape[1:], lambda b, s: (layer, 0, 0), **const),
            pl.BlockSpec((1, D), lambda b, s: (0, 0), **const),
        ],
        out_specs=pl.BlockSpec((1, TS_FFN, D), lambda b, s: (b, s, 0)),
        out_shape=jax.ShapeDtypeStruct((B, S, D), f32),
        scratch_shapes=[
            pltpu.VMEM((TS_FFN, D), bf16),
            pltpu.VMEM((TS_FFN, D), f32),
            pltpu.VMEM((N_FF_CHUNK, 8, 2 * FF_CHUNK), f32),
        ] + [pltpu.VMEM((8 + TS_FFN, 2 * FF_CHUNK), f32)] * FFN_DEPTH + [
            pltpu.VMEM((TS_FFN, FFN_DOWN_GROUP * FF_CHUNK), bf16)] * 2 + [
        ],
        compiler_params=pltpu.CompilerParams(
            dimension_semantics=("arbitrary", "arbitrary"),
            vmem_limit_bytes=48 * 1024 * 1024),
        name="out_ffn_final" if final else "out_ffn",
    )(attn, y, x, w_out, g2, up_w, conv_w, down_w, gf)


def _alibi_bias():
    slopes = 2.0 ** (-8.0 * jnp.arange(1, N_HEADS + 1, dtype=f32) / N_HEADS)
    i = jnp.arange(BAND)[None, :]
    j = jnp.arange(2 * BAND)[:, None]
    dist = BAND + i - j
    valid = (dist >= 0) & (dist <= BAND)
    per_d = []
    for d in DILATIONS:
        b = -(slopes * (d * LOG2E))[:, None, None] * dist.astype(f32)[None]
        b = jnp.where(valid[None], b, NEG).reshape(N_PAIR, 2, 2 * BAND, BAND)
        per_d.append(b.transpose(0, 2, 1, 3).reshape(N_PAIR, 2 * BAND, 2 * BAND))
    return jnp.stack(per_d)


def _group_ones(n):
    g = jnp.arange(n) // HEAD_DIM
    return (g[:, None] == g[None, :]).astype(bf16)


def kernel(x, norm1_g, w_in, mix_conv_w, attn_out_g, conv_out_g, w_out, norm2_g, ffn_up,
           ffn_conv_w, ffn_down, final_norm_g):
    depth = w_in.shape[0]
    bias = _alibi_bias()
    gmat_c = _group_ones(D_CONV)
    gmat_a = _group_ones(LANES)
    gf = final_norm_g.reshape(1, D_MODEL)
    w_in, w_out, ffn_up, ffn_down = (w.astype(bf16) for w in (w_in, w_out, ffn_up, ffn_down))
    for layer in range(depth):
        qkv1, qkv4, qkv16, y = _mixer_in(
            x, norm1_g[layer].reshape(1, D_MODEL), w_in, mix_conv_w[layer],
            conv_out_g[layer].reshape(1, D_CONV), gmat_c, layer)
        attn = _attention(qkv1, qkv4, qkv16, bias,
                          attn_out_g[layer].reshape(N_PAIR, 1, LANES), gmat_a)
        x = _out_ffn(attn, y, x, w_out, norm2_g[layer].reshape(1, D_MODEL),
                     ffn_up, ffn_conv_w[layer], ffn_down, gf, layer, final=(layer == depth - 1))
    return x
```

```python
import functools
import math

import jax
import jax.numpy as jnp
from jax import lax
from jax.experimental import pallas as pl
from jax.experimental.pallas import tpu as pltpu

D_MODEL = 1024
D_ATTN = 512
D_CONV = 512
HEAD_DIM = 64
N_HEADS = D_ATTN // HEAD_DIM
DILATIONS = (1, 4, 16)
BAND = 128
D_FF = 2816
EPS = 1e-6

LANES = 128
N_SLAB = 3 * D_ATTN // LANES
N_PAIR = N_HEADS // 2
FF_CHUNK = 256
N_FF_CHUNK = D_FF // FF_CHUNK
FFN_DEPTH = 3
FFN_DOWN_GROUP = 2
TS = 512
TS_FFN = 512
LOG2E = 1.0 / math.log(2.0)
Q_SCALE = HEAD_DIM ** -0.5 * LOG2E
NEG = -1e30

f32 = jnp.float32
bf16 = jnp.bfloat16


def _rms_scale(x):
    return lax.rsqrt(jnp.mean(x * x, axis=-1, keepdims=True) + EPS)


def _shift_rows(x, prev8, k):
    rolled = pltpu.roll(x, k, axis=0)
    row = lax.broadcasted_iota(jnp.int32, prev8.shape, 0)
    head = jnp.where(row < k, pltpu.roll(prev8, k, axis=0), rolled[:8])
    return jnp.concatenate([head, rolled[8:]], axis=0)


def _mixer_in_kernel(x_ref, g_ref, w_ref, cw_ref, cg_ref, gmat_ref,
                     o1_ref, o4_ref, o16_ref, y_ref, nat_s, d4_s, carry_s, h_s, gb_s, cu_s):
    si = pl.program_id(1)

    @pl.when(si == 0)
    def _():
        carry_s[...] = jnp.zeros_like(carry_s)

    x = x_ref[0]
    h_s[...] = (x * _rms_scale(x) * g_ref[...]).astype(bf16)

    def project(col0):
        return jnp.dot(h_s[...], w_ref[:, col0:col0 + D_ATTN], preferred_element_type=f32)

    c0 = 3 * D_ATTN
    gb_s[...] = project(c0)
    cu_s[:8] = carry_s[...]
    cu_s[8:] = project(c0 + D_CONV) * project(c0 + 2 * D_CONV)
    carry_s[...] = cu_s[TS:]

    def conv_branch():
        cw = cw_ref[...]
        y = gb_s[...] * (cw[0:1] * cu_s[6:6 + TS] + cw[1:2] * cu_s[7:7 + TS] + cw[2:3] * cu_s[8:])
        ss = jnp.dot((y * y).astype(bf16), gmat_ref[...], preferred_element_type=f32)
        y_ref[0] = (y * lax.rsqrt(ss * (1.0 / HEAD_DIM) + EPS) * cg_ref[...]).astype(bf16)

    def qkv_half(half):
        proj = jnp.dot(h_s[...], w_ref[:, half * 2 * LANES:(half + 1) * 2 * LANES],
                       preferred_element_type=f32)
        if half < N_PAIR // 2:
            proj = proj * Q_SCALE
        for p in range(2):
            j = 2 * half + p
            pj = proj[:, p * LANES:(p + 1) * LANES]
            nat_s[j] = pj
            o1_ref[0, j % N_PAIR, j // N_PAIR] = pj.astype(bf16)

    def relayout(half):
        for j in (2 * half, 2 * half + 1):
            for r in range(4):
                c4 = nat_s[j, pl.ds(r, TS // 4, stride=4), :]
                d4_s[j, r] = c4
                o4_ref[0, j % N_PAIR, j // N_PAIR, r] = c4.astype(bf16)
            for r in range(16):
                c16 = d4_s[j, r % 4, pl.ds(r // 4, TS // 16, stride=4), :]
                o16_ref[0, j % N_PAIR, j // N_PAIR, r] = c16.astype(bf16)

    n_half = N_SLAB // 2
    qkv_half(0)
    conv_branch()
    for half in range(1, n_half):
        qkv_half(half)
        relayout(half - 1)
    relayout(n_half - 1)


def _mixer_in(x, g, w_in, conv_w, conv_g, gmat, layer):
    B, S, D = x.shape
    ns = S // TS
    const = dict(pipeline_mode=pl.Buffered(1))
    return pl.pallas_call(
        _mixer_in_kernel,
        grid=(B, ns),
        in_specs=[
            pl.BlockSpec((1, TS, D), lambda b, s: (b, s, 0)),
            pl.BlockSpec((1, D), lambda b, s: (0, 0), **const),
            pl.BlockSpec((None,) + w_in.shape[1:], lambda b, s: (layer, 0, 0), **const),
            pl.BlockSpec(conv_w.shape, lambda b, s: (0, 0), **const),
            pl.BlockSpec((1, D_CONV), lambda b, s: (0, 0), **const),
            pl.BlockSpec(gmat.shape, lambda b, s: (0, 0), **const),
        ],
        out_specs=[
            pl.BlockSpec((1, N_PAIR, 3, TS, LANES), lambda b, s: (b, 0, 0, s, 0)),
            pl.BlockSpec((1, N_PAIR, 3, 4, TS // 4, LANES), lambda b, s: (b, 0, 0, 0, s, 0)),
            pl.BlockSpec((1, N_PAIR, 3, 16, TS // 16, LANES), lambda b, s: (b, 0, 0, 0, s, 0)),
            pl.BlockSpec((1, TS, D_CONV), lambda b, s: (b, s, 0)),
        ],
        out_shape=[
            jax.ShapeDtypeStruct((B, N_PAIR, 3, S, LANES), bf16),
            jax.ShapeDtypeStruct((B, N_PAIR, 3, 4, S // 4, LANES), bf16),
            jax.ShapeDtypeStruct((B, N_PAIR, 3, 16, S // 16, LANES), bf16),
            jax.ShapeDtypeStruct((B, S, D_CONV), bf16),
        ],
        scratch_shapes=[
            pltpu.VMEM((N_SLAB, TS, LANES), f32),
            pltpu.VMEM((N_SLAB, 4, TS // 4, LANES), f32),
            pltpu.VMEM((8, D_CONV), f32),
            pltpu.VMEM((TS, D), bf16),
            pltpu.VMEM((TS, D_CONV), f32),
            pltpu.VMEM((8 + TS, D_CONV), f32),
        ],
        compiler_params=pltpu.CompilerParams(
            dimension_semantics=("arbitrary", "arbitrary"),
            vmem_limit_bytes=40 * 1024 * 1024),
        name="mixer_in",
    )(x, g, w_in, conv_w, conv_g, gmat)


ATTN_DEPTH = 4
ATTN_EMIT_DEPTH = 3


def _attn_scores(q, kcat):
    qt = q.T
    zero = jnp.zeros((HEAD_DIM, BAND), qt.dtype)
    w = jnp.concatenate([jnp.concatenate([qt[:HEAD_DIM], zero], axis=0),
                         jnp.concatenate([zero, qt[HEAD_DIM:]], axis=0)], axis=1)
    return jnp.dot(kcat, w, preferred_element_type=f32)


def _attn_values(st, vt, bias):
    s = st + bias
    m = jnp.max(s, axis=0, keepdims=True)
    p = jnp.exp2(s - m)
    l = jnp.sum(p, axis=0, keepdims=True)
    ot = jnp.dot(vt, p.astype(bf16), preferred_element_type=f32)
    return ot, 1.0 / l, m + jnp.log2(l)


def _attn_rows(ot, rl, lse):
    ot = jnp.concatenate([ot[:HEAD_DIM, :BAND] * rl[:, :BAND],
                          ot[HEAD_DIM:, BAND:] * rl[:, BAND:]], axis=0)
    lset = jnp.concatenate([jnp.broadcast_to(lse[:, :BAND], (HEAD_DIM, BAND)),
                            jnp.broadcast_to(lse[:, BAND:], (HEAD_DIM, BAND))], axis=0)
    return ot.T, lset.T


def _attn_kernel(qkv1_ref, qkv4_ref, qkv16_ref,
                 bias_ref, g_ref, gmat_ref, out_ref, o_s, lse_s, vt_s, st_s, ot_s, stat_s):
    S = qkv1_ref.shape[3]
    pair = pl.program_id(1)

    per_branch = []
    for bi, (d, ref) in enumerate(zip(DILATIONS, (qkv1_ref, qkv4_ref, qkv16_ref))):
        per_branch.append([(bi, d, ref, () if d == 1 else (r,), r, n)
                           for n in range(S // d // BAND) for r in range(d)])
    blocks = [blk for pair in zip(per_branch[2], per_branch[1]) for blk in pair] + per_branch[0]

    nblk = {d: S // d // BAND for d in DILATIONS}

    def flat(blk):
        bi, d, _, _, r, n = blk
        return bi * (S // BAND) + r * nblk[d] + n

    def start(blk, t):
        _, _, ref, cls, _, n = blk
        rows = pl.ds(n * BAND, BAND)
        vt_s[flat(blk)] = ref[(0, 0, 2) + cls + (rows,)].T
        keys = pl.ds((n - 1) * BAND, 2 * BAND) if n else rows
        st = _attn_scores(ref[(0, 0, 0) + cls + (rows,)], ref[(0, 0, 1) + cls + (keys,)])
        st_s[t % st_s.shape[0], :st.shape[0]] = st

    def finish(blk, t):
        bi, _, _, _, _, n = blk
        slot = t % st_s.shape[0]
        if n:
            vt = jnp.concatenate([vt_s[flat(blk) - 1], vt_s[flat(blk)]], axis=1)
            ot, rl, lse = _attn_values(st_s[slot], vt, bias_ref[bi, pair])
        else:
            ot, rl, lse = _attn_values(st_s[slot, :BAND], vt_s[flat(blk)],
                                       bias_ref[bi, pair, BAND:, :])
        slot = t % ot_s.shape[0]
        ot_s[slot] = ot
        stat_s[slot, 0:1] = rl
        stat_s[slot, 1:2] = lse

    def emit(blk, t):
        bi, d, _, _, r, n = blk
        slot = t % ot_s.shape[0]
        o, lse = _attn_rows(ot_s[slot], stat_s[slot, 0:1], stat_s[slot, 1:2])
        rows = pl.ds(n * (BAND * d) + r, BAND, stride=d) if d > 1 else pl.ds(n * BAND, BAND)
        o_s[bi, rows, :] = o
        lse_s[bi, rows, :] = lse

    def merge(j):
        rows = pl.ds(j * BAND, BAND)
        l1, l4, l16 = lse_s[0, rows, :], lse_s[1, rows, :], lse_s[2, rows, :]
        lm = jnp.maximum(jnp.maximum(l1, l4), l16)
        w1, w4, w16 = jnp.exp2(l1 - lm), jnp.exp2(l4 - lm), jnp.exp2(l16 - lm)
        num = o_s[0, rows, :] * w1 + o_s[1, rows, :] * w4 + o_s[2, rows, :] * w16
        a = num / (w1 + w4 + w16)
        ss = jnp.dot((a * a).astype(bf16), gmat_ref[...], preferred_element_type=f32)
        an = a * lax.rsqrt(ss * (1.0 / HEAD_DIM) + EPS) * g_ref[0]
        out_ref[0, 0, rows, :] = an.astype(bf16)

    for t in range(len(blocks) + ATTN_DEPTH + ATTN_EMIT_DEPTH):
        if t < len(blocks):
            start(blocks[t], t)
        if 0 <= t - ATTN_DEPTH < len(blocks):
            finish(blocks[t - ATTN_DEPTH], t - ATTN_DEPTH)
        done = t - ATTN_DEPTH - ATTN_EMIT_DEPTH
        if 0 <= done:
            emit(blocks[done], done)
            if blocks[done][0] == 0:
                merge(blocks[done][5])


def _attention(qkv1, qkv4, qkv16, bias, attn_g, gmat):
    B, _, _, S, _ = qkv1.shape
    return pl.pallas_call(
        _attn_kernel,
        grid=(B, N_PAIR),
        in_specs=[pl.BlockSpec((1, 1, 3, S, LANES), lambda b, p: (b, p, 0, 0, 0)),
                  pl.BlockSpec((1, 1, 3, 4, S // 4, LANES), lambda b, p: (b, p, 0, 0, 0, 0)),
                  pl.BlockSpec((1, 1, 3, 16, S // 16, LANES), lambda b, p: (b, p, 0, 0, 0, 0)),
                  pl.BlockSpec(bias.shape, lambda b, p: (0, 0, 0, 0), pipeline_mode=pl.Buffered(1)),
                  pl.BlockSpec((1, 1, LANES), lambda b, p: (p, 0, 0)),
                  pl.BlockSpec(gmat.shape, lambda b, p: (0, 0), pipeline_mode=pl.Buffered(1))],
        out_specs=pl.BlockSpec((1, 1, S, LANES), lambda b, p: (b, p, 0, 0)),
        out_shape=jax.ShapeDtypeStruct((B, N_PAIR, S, LANES), bf16),
        scratch_shapes=[pltpu.VMEM((3, S, LANES), f32), pltpu.VMEM((3, S, LANES), f32),
                        pltpu.VMEM((3 * S // BAND, LANES, BAND), bf16),
                        pltpu.VMEM((ATTN_DEPTH + 1, 2 * BAND, 2 * BAND), f32),
                        pltpu.VMEM((ATTN_EMIT_DEPTH + 1, LANES, 2 * BAND), f32),
                        pltpu.VMEM((ATTN_EMIT_DEPTH + 1, 8, 2 * BAND), f32)],
        compiler_params=pltpu.CompilerParams(
            dimension_semantics=("arbitrary", "arbitrary"),
            vmem_limit_bytes=40 * 1024 * 1024),
        name="dilated_attention",
    )(qkv1, qkv4, qkv16, bias, attn_g, gmat)


def _out_ffn_kernel(a_ref, y_ref, x_ref, wo_ref, g2_ref, up_ref, cw_ref, down_ref, gf_ref,
                    out_ref, h2_s, acc_s, carry_s, *rings, final):
    up_slots, act_slots = rings[:FFN_DEPTH], rings[FFN_DEPTH:]
    si = pl.program_id(1)
    mix = jnp.concatenate([a_ref[0, p] for p in range(N_PAIR)] + [y_ref[0]], axis=1)
    x1 = x_ref[0] + jnp.dot(mix, wo_ref[...], preferred_element_type=f32)
    h2_s[...] = (x1 * _rms_scale(x1) * g2_ref[...]).astype(bf16)
    acc_s[...] = x1

    @pl.when(si == 0)
    def _():
        carry_s[...] = jnp.zeros_like(carry_s)

    def halves(c):
        return [pl.ds(half * D_FF + c * FF_CHUNK, FF_CHUNK) for half in range(2)]

    def up_proj(c):
        up_s = up_slots[c % FFN_DEPTH]
        up_s[:8] = carry_s[c]
        for half, cols in enumerate(halves(c)):
            up_s[8:, half * FF_CHUNK:(half + 1) * FF_CHUNK] = jnp.dot(
                h2_s[...], up_ref[:, cols], preferred_element_type=f32)
        carry_s[c] = up_s[TS_FFN:]

    def activate(c):
        up_s = up_slots[c % FFN_DEPTH]
        cw = jnp.concatenate([cw_ref[:, cols] for cols in halves(c)], axis=1)
        conv = cw[0:1] * up_s[6:6 + TS_FFN] + cw[1:2] * up_s[7:7 + TS_FFN] + cw[2:3] * up_s[8:]
        gate, val = conv[:, :FF_CHUNK], conv[:, FF_CHUNK:]
        act = (gate / (1.0 + jnp.exp(-gate)) * val).astype(bf16)
        g = FFN_DOWN_GROUP
        act_slots[c // g % 2][:, c % g * FF_CHUNK:(c % g + 1) * FF_CHUNK] = act

    def down_proj(c):
        first = c - c % FFN_DOWN_GROUP
        width = (c - first + 1) * FF_CHUNK
        acc_s[...] += jnp.dot(act_slots[c // FFN_DOWN_GROUP % 2][:, :width],
                              down_ref[pl.ds(first * FF_CHUNK, width), :],
                              preferred_element_type=f32)

    for t in range(N_FF_CHUNK + FFN_DEPTH):
        if t < N_FF_CHUNK:
            up_proj(t)
        if 0 <= t - (FFN_DEPTH - 1) < N_FF_CHUNK:
            activate(t - (FFN_DEPTH - 1))
        c = t - FFN_DEPTH
        if 0 <= c and ((c + 1) % FFN_DOWN_GROUP == 0 or c == N_FF_CHUNK - 1):
            down_proj(c)

    out = acc_s[...]
    if final:
        out = out * _rms_scale(out) * gf_ref[...]
    out_ref[0] = out


def _out_ffn(attn, y, x, w_out, g2, up_w, conv_w, down_w, gf, layer, final):
    B, S, D = x.shape
    ns = S // TS_FFN
    const = dict(pipeline_mode=pl.Buffered(1))
    return pl.pallas_call(
        functools.partial(_out_ffn_kernel, final=final),
        grid=(B, ns),
        in_specs=[
            pl.BlockSpec((1, N_PAIR, TS_FFN, LANES), lambda b, s: (b, 0, s, 0)),
            pl.BlockSpec((1, TS_FFN, D_CONV), lambda b, s: (b, s, 0)),
            pl.BlockSpec((1, TS_FFN, D), lambda b, s: (b, s, 0)),
            pl.BlockSpec((None,) + w_out.shape[1:], lambda b, s: (layer, 0, 0), **const),
            pl.BlockSpec((1, D), lambda b, s: (0, 0), **const),
            pl.BlockSpec((None,) + up_w.shape[1:], lambda b, s: (layer, 0, 0), **const),
            pl.BlockSpec(conv_w.shape, lambda b, s: (0, 0), **const),
            pl.BlockSpec((None,) + down_w.shape[1:], lambda b, s: (layer, 0, 0), **const),
            pl.BlockSpec((1, D), lambda b, s: (0, 0), **const),
        ],
        out_specs=pl.BlockSpec((1, TS_FFN, D), lambda b, s: (b, s, 0)),
        out_shape=jax.ShapeDtypeStruct((B, S, D), f32),
        scratch_shapes=[
            pltpu.VMEM((TS_FFN, D), bf16),
            pltpu.VMEM((TS_FFN, D), f32),
            pltpu.VMEM((N_FF_CHUNK, 8, 2 * FF_CHUNK), f32),
        ] + [pltpu.VMEM((8 + TS_FFN, 2 * FF_CHUNK), f32)] * FFN_DEPTH + [
            pltpu.VMEM((TS_FFN, FFN_DOWN_GROUP * FF_CHUNK), bf16)] * 2 + [
        ],
        compiler_params=pltpu.CompilerParams(
            dimension_semantics=("arbitrary", "arbitrary"),
            vmem_limit_bytes=48 * 1024 * 1024),
        name="out_ffn_final" if final else "out_ffn",
    )(attn, y, x, w_out, g2, up_w, conv_w, down_w, gf)


def _alibi_bias():
    slopes = 2.0 ** (-8.0 * jnp.arange(1, N_HEADS + 1, dtype=f32) / N_HEADS)
    i = jnp.arange(BAND)[None, :]
    j = jnp.arange(2 * BAND)[:, None]
    dist = BAND + i - j
    valid = (dist >= 0) & (dist <= BAND)
    per_d = []
    for d in DILATIONS:
        b = -(slopes * (d * LOG2E))[:, None, None] * dist.astype(f32)[None]
        b = jnp.where(valid[None], b, NEG).reshape(N_PAIR, 2, 2 * BAND, BAND)
        per_d.append(b.transpose(0, 2, 1, 3).reshape(N_PAIR, 2 * BAND, 2 * BAND))
    return jnp.stack(per_d)


def _group_ones(n):
    g = jnp.arange(n) // HEAD_DIM
    return (g[:, None] == g[None, :]).astype(bf16)


def kernel(x, norm1_g, w_in, mix_conv_w, attn_out_g, conv_out_g, w_out, norm2_g, ffn_up,
           ffn_conv_w, ffn_down, final_norm_g):
    depth = w_in.shape[0]
    bias = _alibi_bias()
    gmat_c = _group_ones(D_CONV)
    gmat_a = _group_ones(LANES)
    gf = final_norm_g.reshape(1, D_MODEL)
    w_in, w_out, ffn_up, ffn_down = (w.astype(bf16) for w in (w_in, w_out, ffn_up, ffn_down))
    for layer in range(depth):
        qkv1, qkv4, qkv16, y = _mixer_in(
            x, norm1_g[layer].reshape(1, D_MODEL), w_in, mix_conv_w[layer],
            conv_out_g[layer].reshape(1, D_CONV), gmat_c, layer)
        attn = _attention(qkv1, qkv4, qkv16, bias,
                          attn_out_g[layer].reshape(N_PAIR, 1, LANES), gmat_a)
        x = _out_ffn(attn, y, x, w_out, norm2_g[layer].reshape(1, D_MODEL),
                     ffn_up, ffn_conv_w[layer], ffn_down, gf, layer, final=(layer == depth - 1))
    return x
```

```python
import functools
import math

import jax
import jax.numpy as jnp
from jax import lax
from jax.experimental import pallas as pl
from jax.experimental.pallas import tpu as pltpu

D_MODEL = 1024
D_ATTN = 512
D_CONV = 512
HEAD_DIM = 64
N_HEADS = D_ATTN // HEAD_DIM
DILATIONS = (1, 4, 16)
BAND = 128
D_FF = 2816
EPS = 1e-6

LANES = 128
N_SLAB = 3 * D_ATTN // LANES
N_PAIR = N_HEADS // 2
FF_CHUNK = 256
N_FF_CHUNK = D_FF // FF_CHUNK
FFN_DEPTH = 3
FFN_DOWN_GROUP = 2
TS = 512
TS_FFN = 512
LOG2E = 1.0 / math.log(2.0)
Q_SCALE = HEAD_DIM ** -0.5 * LOG2E
NEG = -1e30

f32 = jnp.float32
bf16 = jnp.bfloat16


def _rms_scale(x):
    return lax.rsqrt(jnp.mean(x * x, axis=-1, keepdims=True) + EPS)


def _mixer_in_kernel(x_ref, g_ref, w_ref, cw_ref, cg_ref, gmat_ref,
                     o1_ref, o4_ref, o16_ref, y_ref, nat_s, d4_s, carry_s, h_s, gb_s, cu_s):
    si = pl.program_id(1)

    @pl.when(si == 0)
    def _():
        carry_s[...] = jnp.zeros_like(carry_s)

    x = x_ref[0]
    h_s[...] = (x * _rms_scale(x) * g_ref[...]).astype(bf16)

    def project(col0):
        return jnp.dot(h_s[...], w_ref[:, col0:col0 + D_ATTN], preferred_element_type=f32)

    c0 = 3 * D_ATTN
    gb_s[...] = project(c0)
    cu_s[:8] = carry_s[...]
    cu_s[8:] = project(c0 + D_CONV) * project(c0 + 2 * D_CONV)
    carry_s[...] = cu_s[TS:]

    def conv_branch():
        cw = cw_ref[...]
        y = gb_s[...] * (cw[0:1] * cu_s[6:6 + TS] + cw[1:2] * cu_s[7:7 + TS] + cw[2:3] * cu_s[8:])
        ss = jnp.dot((y * y).astype(bf16), gmat_ref[...], preferred_element_type=f32)
        y_ref[0] = (y * lax.rsqrt(ss * (1.0 / HEAD_DIM) + EPS) * cg_ref[...]).astype(bf16)

    def qkv_half(half):
        proj = jnp.dot(h_s[...], w_ref[:, half * 2 * LANES:(half + 1) * 2 * LANES],
                       preferred_element_type=f32)
        if half < N_PAIR // 2:
            proj = proj * Q_SCALE
        for p in range(2):
            j = 2 * half + p
            pj = proj[:, p * LANES:(p + 1) * LANES]
            nat_s[j] = pj
            o1_ref[0, j % N_PAIR, j // N_PAIR] = pj.astype(bf16)

    def relayout(half):
        for j in (2 * half, 2 * half + 1):
            for r in range(4):
                c4 = nat_s[j, pl.ds(r, TS // 4, stride=4), :]
                d4_s[j, r] = c4
                o4_ref[0, j % N_PAIR, j // N_PAIR, r] = c4.astype(bf16)
            for r in range(16):
                c16 = d4_s[j, r % 4, pl.ds(r // 4, TS // 16, stride=4), :]
                o16_ref[0, j % N_PAIR, j // N_PAIR, r] = c16.astype(bf16)

    n_half = N_SLAB // 2
    qkv_half(0)
    conv_branch()
    for half in range(1, n_half):
        qkv_half(half)
        relayout(half - 1)
    relayout(n_half - 1)


def _mixer_in(x, g, w_in, conv_w, conv_g, gmat, layer):
    B, S, D = x.shape
    ns = S // TS
    const = dict(pipeline_mode=pl.Buffered(1))
    return pl.pallas_call(
        _mixer_in_kernel,
        grid=(B, ns),
        in_specs=[
            pl.BlockSpec((1, TS, D), lambda b, s: (b, s, 0)),
            pl.BlockSpec((1, D), lambda b, s: (0, 0), **const),
            pl.BlockSpec((None,) + w_in.shape[1:], lambda b, s: (layer, 0, 0), **const),
            pl.BlockSpec(conv_w.shape, lambda b, s: (0, 0), **const),
            pl.BlockSpec((1, D_CONV), lambda b, s: (0, 0), **const),
            pl.BlockSpec(gmat.shape, lambda b, s: (0, 0), **const),
        ],
        out_specs=[
            pl.BlockSpec((1, N_PAIR, 3, TS, LANES), lambda b, s: (b, 0, 0, s, 0)),
            pl.BlockSpec((1, N_PAIR, 3, 4, TS // 4, LANES), lambda b, s: (b, 0, 0, 0, s, 0)),
            pl.BlockSpec((1, N_PAIR, 3, 16, TS // 16, LANES), lambda b, s: (b, 0, 0, 0, s, 0)),
            pl.BlockSpec((1, TS, D_CONV), lambda b, s: (b, s, 0)),
        ],
        out_shape=[
            jax.ShapeDtypeStruct((B, N_PAIR, 3, S, LANES), bf16),
            jax.ShapeDtypeStruct((B, N_PAIR, 3, 4, S // 4, LANES), bf16),
            jax.ShapeDtypeStruct((B, N_PAIR, 3, 16, S // 16, LANES), bf16),
            jax.ShapeDtypeStruct((B, S, D_CONV), bf16),
        ],
        scratch_shapes=[
            pltpu.VMEM((N_SLAB, TS, LANES), f32),
            pltpu.VMEM((N_SLAB, 4, TS // 4, LANES), f32),
            pltpu.VMEM((8, D_CONV), f32),
            pltpu.VMEM((TS, D), bf16),
            pltpu.VMEM((TS, D_CONV), f32),
            pltpu.VMEM((8 + TS, D_CONV), f32),
        ],
        compiler_params=pltpu.CompilerParams(
            dimension_semantics=("arbitrary", "arbitrary"),
            vmem_limit_bytes=40 * 1024 * 1024),
        name="mixer_in",
    )(x, g, w_in, conv_w, conv_g, gmat)


ATTN_DEPTH = 4
ATTN_EMIT_DEPTH = 3


def _attn_scores(q, kcat):
    qt = q.T
    zero = jnp.zeros((HEAD_DIM, BAND), qt.dtype)
    w = jnp.concatenate([jnp.concatenate([qt[:HEAD_DIM], zero], axis=0),
                         jnp.concatenate([zero, qt[HEAD_DIM:]], axis=0)], axis=1)
    return jnp.dot(kcat, w, preferred_element_type=f32)


def _attn_values(st, vt, bias):
    s = st + bias
    m = jnp.max(s, axis=0, keepdims=True)
    p = jnp.exp2(s - m)
    l = jnp.sum(p, axis=0, keepdims=True)
    ot = jnp.dot(vt, p.astype(bf16), preferred_element_type=f32)
    return ot, 1.0 / l, m + jnp.log2(l)


def _attn_rows(ot, rl, lse):
    ot = jnp.concatenate([ot[:HEAD_DIM, :BAND] * rl[:, :BAND],
                          ot[HEAD_DIM:, BAND:] * rl[:, BAND:]], axis=0)
    lset = jnp.concatenate([jnp.broadcast_to(lse[:, :BAND], (HEAD_DIM, BAND)),
                            jnp.broadcast_to(lse[:, BAND:], (HEAD_DIM, BAND))], axis=0)
    return ot.T, lset.T


def _attn_kernel(qkv1_ref, qkv4_ref, qkv16_ref,
                 bias_ref, g_ref, gmat_ref, out_ref, o_s, lse_s, vt_s, st_s, ot_s, stat_s):
    S = qkv1_ref.shape[3]
    pair = pl.program_id(1)

    per_branch = []
    for bi, (d, ref) in enumerate(zip(DILATIONS, (qkv1_ref, qkv4_ref, qkv16_ref))):
        per_branch.append([(bi, d, ref, () if d == 1 else (r,), r, n)
                           for n in range(S // d // BAND) for r in range(d)])
    blocks = [blk for pair in zip(per_branch[2], per_branch[1]) for blk in pair] + per_branch[0]

    nblk = {d: S // d // BAND for d in DILATIONS}

    def flat(blk):
        bi, d, _, _, r, n = blk
        return bi * (S // BAND) + r * nblk[d] + n

    def start(blk, t):
        _, _, ref, cls, _, n = blk
        rows = pl.ds(n * BAND, BAND)
        vt_s[flat(blk)] = ref[(0, 0, 2) + cls + (rows,)].T
        keys = pl.ds((n - 1) * BAND, 2 * BAND) if n else rows
        st = _attn_scores(ref[(0, 0, 0) + cls + (rows,)], ref[(0, 0, 1) + cls + (keys,)])
        st_s[t % st_s.shape[0], :st.shape[0]] = st

    def finish(blk, t):
        bi, _, _, _, _, n = blk
        slot = t % st_s.shape[0]
        if n:
            vt = jnp.concatenate([vt_s[flat(blk) - 1], vt_s[flat(blk)]], axis=1)
            ot, rl, lse = _attn_values(st_s[slot], vt, bias_ref[bi, pair])
        else:
            ot, rl, lse = _attn_values(st_s[slot, :BAND], vt_s[flat(blk)],
                                       bias_ref[bi, pair, BAND:, :])
        slot = t % ot_s.shape[0]
        ot_s[slot] = ot
        stat_s[slot, 0:1] = rl
        stat_s[slot, 1:2] = lse

    def emit(blk, t):
        bi, d, _, _, r, n = blk
        slot = t % ot_s.shape[0]
        o, lse = _attn_rows(ot_s[slot], stat_s[slot, 0:1], stat_s[slot, 1:2])
        if d == 16:
            rows, dst = pl.ds((r % 4) * (S // 4) + r // 4, BAND, stride=4), 3
        else:
            rows, dst = (pl.ds(n * (BAND * d) + r, BAND, stride=d) if d > 1
                         else pl.ds(n * BAND, BAND)), bi
        o_s[dst, rows, :] = o
        lse_s[dst, rows, :] = lse

    def untangle(r4, m):
        src = pl.ds(r4 * (S // 4) + m * BAND, BAND)
        dst = pl.ds(m * (4 * BAND) + r4, BAND, stride=4)
        o_s[2, dst, :] = o_s[3, src, :]
        lse_s[2, dst, :] = lse_s[3, src, :]

    def merge(j):
        rows = pl.ds(j * BAND, BAND)
        l1, l4, l16 = lse_s[0, rows, :], lse_s[1, rows, :], lse_s[2, rows, :]
        lm = jnp.maximum(jnp.maximum(l1, l4), l16)
        w1, w4, w16 = jnp.exp2(l1 - lm), jnp.exp2(l4 - lm), jnp.exp2(l16 - lm)
        num = o_s[0, rows, :] * w1 + o_s[1, rows, :] * w4 + o_s[2, rows, :] * w16
        a = num / (w1 + w4 + w16)
        ss = jnp.dot((a * a).astype(bf16), gmat_ref[...], preferred_element_type=f32)
        an = a * lax.rsqrt(ss * (1.0 / HEAD_DIM) + EPS) * g_ref[0]
        out_ref[0, 0, rows, :] = an.astype(bf16)

    for t in range(len(blocks) + ATTN_DEPTH + ATTN_EMIT_DEPTH):
        if t < len(blocks):
            start(blocks[t], t)
        if 0 <= t - ATTN_DEPTH < len(blocks):
            finish(blocks[t - ATTN_DEPTH], t - ATTN_DEPTH)
        done = t - ATTN_DEPTH - ATTN_EMIT_DEPTH
        if 0 <= done:
            emit(blocks[done], done)
            if done == len(per_branch[2]) + len(per_branch[1]) - 1:
                for r4 in range(4):
                    for m in range(S // 4 // BAND):
                        untangle(r4, m)
            if blocks[done][0] == 0:
                merge(blocks[done][5])


def _attention(qkv1, qkv4, qkv16, bias, attn_g, gmat):
    B, _, _, S, _ = qkv1.shape
    return pl.pallas_call(
        _attn_kernel,
        grid=(B, N_PAIR),
        in_specs=[pl.BlockSpec((1, 1, 3, S, LANES), lambda b, p: (b, p, 0, 0, 0)),
                  pl.BlockSpec((1, 1, 3, 4, S // 4, LANES), lambda b, p: (b, p, 0, 0, 0, 0)),
                  pl.BlockSpec((1, 1, 3, 16, S // 16, LANES), lambda b, p: (b, p, 0, 0, 0, 0)),
                  pl.BlockSpec(bias.shape, lambda b, p: (0, 0, 0, 0), pipeline_mode=pl.Buffered(1)),
                  pl.BlockSpec((1, 1, LANES), lambda b, p: (p, 0, 0)),
                  pl.BlockSpec(gmat.shape, lambda b, p: (0, 0), pipeline_mode=pl.Buffered(1))],
        out_specs=pl.BlockSpec((1, 1, S, LANES), lambda b, p: (b, p, 0, 0)),
        out_shape=jax.ShapeDtypeStruct((B, N_PAIR, S, LANES), bf16),
        scratch_shapes=[pltpu.VMEM((4, S, LANES), f32), pltpu.VMEM((4, S, LANES), f32),
                        pltpu.VMEM((3 * S // BAND, LANES, BAND), bf16),
                        pltpu.VMEM((ATTN_DEPTH + 1, 2 * BAND, 2 * BAND), f32),
                        pltpu.VMEM((ATTN_EMIT_DEPTH + 1, LANES, 2 * BAND), f32),
                        pltpu.VMEM((ATTN_EMIT_DEPTH + 1, 8, 2 * BAND), f32)],
        compiler_params=pltpu.CompilerParams(
            dimension_semantics=("arbitrary", "arbitrary"),
            vmem_limit_bytes=40 * 1024 * 1024),
        name="dilated_attention",
    )(qkv1, qkv4, qkv16, bias, attn_g, gmat)


def _out_ffn_kernel(a_ref, y_ref, x_ref, wo_ref, g2_ref, up_ref, cw_ref, down_ref, gf_ref,
                    out_ref, h2_s, acc_s, carry_s, *rings, final):
    up_slots, act_slots = rings[:FFN_DEPTH], rings[FFN_DEPTH:]
    si = pl.program_id(1)
    mix = jnp.concatenate([a_ref[0, p] for p in range(N_PAIR)] + [y_ref[0]], axis=1)
    x1 = x_ref[0] + jnp.dot(mix, wo_ref[...], preferred_element_type=f32)
    h2_s[...] = (x1 * _rms_scale(x1) * g2_ref[...]).astype(bf16)
    acc_s[...] = x1

    @pl.when(si == 0)
    def _():
        carry_s[...] = jnp.zeros_like(carry_s)

    def halves(c):
        return [pl.ds(half * D_FF + c * FF_CHUNK, FF_CHUNK) for half in range(2)]

    def up_proj(c):
        up_s = up_slots[c % FFN_DEPTH]
        up_s[:8] = carry_s[c]
        for half, cols in enumerate(halves(c)):
            up_s[8:, half * FF_CHUNK:(half + 1) * FF_CHUNK] = jnp.dot(
                h2_s[...], up_ref[:, cols], preferred_element_type=f32)
        carry_s[c] = up_s[TS_FFN:]

    def activate(c):
        up_s = up_slots[c % FFN_DEPTH]
        cw = jnp.concatenate([cw_ref[:, cols] for cols in halves(c)], axis=1)
        conv = cw[0:1] * up_s[6:6 + TS_FFN] + cw[1:2] * up_s[7:7 + TS_FFN] + cw[2:3] * up_s[8:]
        gate, val = conv[:, :FF_CHUNK], conv[:, FF_CHUNK:]
        act = (gate / (1.0 + jnp.exp(-gate)) * val).astype(bf16)
        g = FFN_DOWN_GROUP
        act_slots[c // g % 2][:, c % g * FF_CHUNK:(c % g + 1) * FF_CHUNK] = act

    def down_proj(c):
        first = c - c % FFN_DOWN_GROUP
        width = (c - first + 1) * FF_CHUNK
        acc_s[...] += jnp.dot(act_slots[c // FFN_DOWN_GROUP % 2][:, :width],
                              down_ref[pl.ds(first * FF_CHUNK, width), :],
                              preferred_element_type=f32)

    for t in range(N_FF_CHUNK + FFN_DEPTH):
        if t < N_FF_CHUNK:
            up_proj(t)
        if 0 <= t - (FFN_DEPTH - 1) < N_FF_CHUNK:
            activate(t - (FFN_DEPTH - 1))
        c = t - FFN_DEPTH
        if 0 <= c and ((c + 1) % FFN_DOWN_GROUP == 0 or c == N_FF_CHUNK - 1):
            down_proj(c)

    out = acc_s[...]
    if final:
        out = out * _rms_scale(out) * gf_ref[...]
    out_ref[0] = out


def _out_ffn(attn, y, x, w_out, g2, up_w, conv_w, down_w, gf, layer, final):
    B, S, D = x.shape
    ns = S // TS_FFN
    const = dict(pipeline_mode=pl.Buffered(1))
    return pl.pallas_call(
        functools.partial(_out_ffn_kernel, final=final),
        grid=(B, ns),
        in_specs=[
            pl.BlockSpec((1, N_PAIR, TS_FFN, LANES), lambda b, s: (b, 0, s, 0)),
            pl.BlockSpec((1, TS_FFN, D_CONV), lambda b, s: (b, s, 0)),
            pl.BlockSpec((1, TS_FFN, D), lambda b, s: (b, s, 0)),
            pl.BlockSpec((None,) + w_out.shape[1:], lambda b, s: (layer, 0, 0), **const),
            pl.BlockSpec((1, D), lambda b, s: (0, 0), **const),
            pl.BlockSpec((None,) + up_w.shape[1:], lambda b, s: (layer, 0, 0), **const),
            pl.BlockSpec(conv_w.shape, lambda b, s: (0, 0), **const),
            pl.BlockSpec((None,) + down_w.shape[1:], lambda b, s: (layer, 0, 0), **const),
            pl.BlockSpec((1, D), lambda b, s: (0, 0), **const),
        ],
        out_specs=pl.BlockSpec((1, TS_FFN, D), lambda b, s: (b, s, 0)),
        out_shape=jax.ShapeDtypeStruct((B, S, D), f32),
        scratch_shapes=[
            pltpu.VMEM((TS_FFN, D), bf16),
            pltpu.VMEM((TS_FFN, D), f32),
            pltpu.VMEM((N_FF_CHUNK, 8, 2 * FF_CHUNK), f32),
        ] + [pltpu.VMEM((8 + TS_FFN, 2 * FF_CHUNK), f32)] * FFN_DEPTH + [
            pltpu.VMEM((TS_FFN, FFN_DOWN_GROUP * FF_CHUNK), bf16)] * 2 + [
        ],
        compiler_params=pltpu.CompilerParams(
            dimension_semantics=("arbitrary", "arbitrary"),
            vmem_limit_bytes=48 * 1024 * 1024),
        name="out_ffn_final" if final else "out_ffn",
    )(attn, y, x, w_out, g2, up_w, conv_w, down_w, gf)


def _alibi_bias():
    slopes = 2.0 ** (-8.0 * jnp.arange(1, N_HEADS + 1, dtype=f32) / N_HEADS)
    i = jnp.arange(BAND)[None, :]
    j = jnp.arange(2 * BAND)[:, None]
    dist = BAND + i - j
    valid = (dist >= 0) & (dist <= BAND)
    per_d = []
    for d in DILATIONS:
        b = -(slopes * (d * LOG2E))[:, None, None] * dist.astype(f32)[None]
        b = jnp.where(valid[None], b, NEG).reshape(N_PAIR, 2, 2 * BAND, BAND)
        per_d.append(b.transpose(0, 2, 1, 3).reshape(N_PAIR, 2 * BAND, 2 * BAND))
    return jnp.stack(per_d)


def _group_ones(n):
    g = jnp.arange(n) // HEAD_DIM
    return (g[:, None] == g[None, :]).astype(bf16)


def kernel(x, norm1_g, w_in, mix_conv_w, attn_out_g, conv_out_g, w_out, norm2_g, ffn_up,
           ffn_conv_w, ffn_down, final_norm_g):
    depth = w_in.shape[0]
    bias = _alibi_bias()
    gmat_c = _group_ones(D_CONV)
    gmat_a = _group_ones(LANES)
    gf = final_norm_g.reshape(1, D_MODEL)
    w_in, w_out, ffn_up, ffn_down = (w.astype(bf16) for w in (w_in, w_out, ffn_up, ffn_down))
    for layer in range(depth):
        qkv1, qkv4, qkv16, y = _mixer_in(
            x, norm1_g[layer].reshape(1, D_MODEL), w_in, mix_conv_w[layer],
            conv_out_g[layer].reshape(1, D_CONV), gmat_c, layer)
        attn = _attention(qkv1, qkv4, qkv16, bias,
                          attn_out_g[layer].reshape(N_PAIR, 1, LANES), gmat_a)
        x = _out_ffn(attn, y, x, w_out, norm2_g[layer].reshape(1, D_MODEL),
                     ffn_up, ffn_conv_w[layer], ffn_down, gf, layer, final=(layer == depth - 1))
    return x
```

```python
import functools
import math

import jax
import jax.numpy as jnp
from jax import lax
from jax.experimental import pallas as pl
from jax.experimental.pallas import tpu as pltpu

D_MODEL = 1024
D_ATTN = 512
D_CONV = 512
HEAD_DIM = 64
N_HEADS = D_ATTN // HEAD_DIM
DILATIONS = (1, 4, 16)
BAND = 128
D_FF = 2816
EPS = 1e-6

LANES = 128
N_SLAB = 3 * D_ATTN // LANES
N_PAIR = N_HEADS // 2
FF_CHUNK = 256
N_FF_CHUNK = D_FF // FF_CHUNK
FFN_DEPTH = 3
FFN_DOWN_GROUP = 2
TS = 512
TS_FFN = 512
LOG2E = 1.0 / math.log(2.0)
Q_SCALE = HEAD_DIM ** -0.5 * LOG2E
NEG = -1e30

f32 = jnp.float32
bf16 = jnp.bfloat16


def _rms_scale(x):
    return lax.rsqrt(jnp.mean(x * x, axis=-1, keepdims=True) + EPS)


def _mixer_in_kernel(x_ref, g_ref, w_ref, cw_ref, cg_ref, gmat_ref,
                     o1_ref, o4_ref, o16_ref, y_ref, nat_s, d4_s, carry_s, h_s, gb_s, cu_s):
    si = pl.program_id(1)

    @pl.when(si == 0)
    def _():
        carry_s[...] = jnp.zeros_like(carry_s)

    x = x_ref[0]
    h_s[...] = (x * _rms_scale(x) * g_ref[...]).astype(bf16)

    def project(col0):
        return jnp.dot(h_s[...], w_ref[:, col0:col0 + D_ATTN], preferred_element_type=f32)

    c0 = 3 * D_ATTN
    gb_s[...] = project(c0)
    cu_s[:8] = carry_s[...]
    cu_s[8:] = project(c0 + D_CONV) * project(c0 + 2 * D_CONV)
    carry_s[...] = cu_s[TS:]

    def conv_branch():
        cw = cw_ref[...]
        y = gb_s[...] * (cw[0:1] * cu_s[6:6 + TS] + cw[1:2] * cu_s[7:7 + TS] + cw[2:3] * cu_s[8:])
        ss = jnp.dot((y * y).astype(bf16), gmat_ref[...], preferred_element_type=f32)
        y_ref[0] = (y * lax.rsqrt(ss * (1.0 / HEAD_DIM) + EPS) * cg_ref[...]).astype(bf16)

    def qkv_half(half):
        proj = jnp.dot(h_s[...], w_ref[:, half * 2 * LANES:(half + 1) * 2 * LANES],
                       preferred_element_type=f32)
        if half < N_PAIR // 2:
            proj = proj * Q_SCALE
        for p in range(2):
            j = 2 * half + p
            pj = proj[:, p * LANES:(p + 1) * LANES]
            nat_s[j] = pj
            o1_ref[0, j % N_PAIR, j // N_PAIR] = pj.astype(bf16)

    def relayout(half):
        for j in (2 * half, 2 * half + 1):
            for r in range(4):
                c4 = nat_s[j, pl.ds(r, TS // 4, stride=4), :]
                d4_s[j, r] = c4
                o4_ref[0, j % N_PAIR, j // N_PAIR, r] = c4.astype(bf16)
            for r in range(16):
                c16 = d4_s[j, r % 4, pl.ds(r // 4, TS // 16, stride=4), :]
                o16_ref[0, j % N_PAIR, j // N_PAIR, r] = c16.astype(bf16)

    n_half = N_SLAB // 2
    qkv_half(0)
    conv_branch()
    for half in range(1, n_half):
        qkv_half(half)
        relayout(half - 1)
    relayout(n_half - 1)


def _mixer_in(x, g, w_in, conv_w, conv_g, gmat, layer):
    B, S, D = x.shape
    ns = S // TS
    const = dict(pipeline_mode=pl.Buffered(1))
    return pl.pallas_call(
        _mixer_in_kernel,
        grid=(B, ns),
        in_specs=[
            pl.BlockSpec((1, TS, D), lambda b, s: (b, s, 0)),
            pl.BlockSpec((1, D), lambda b, s: (0, 0), **const),
            pl.BlockSpec((None,) + w_in.shape[1:], lambda b, s: (layer, 0, 0), **const),
            pl.BlockSpec(conv_w.shape, lambda b, s: (0, 0), **const),
            pl.BlockSpec((1, D_CONV), lambda b, s: (0, 0), **const),
            pl.BlockSpec(gmat.shape, lambda b, s: (0, 0), **const),
        ],
        out_specs=[
            pl.BlockSpec((1, N_PAIR, 3, TS, LANES), lambda b, s: (b, 0, 0, s, 0)),
            pl.BlockSpec((1, N_PAIR, 3, 4, TS // 4, LANES), lambda b, s: (b, 0, 0, 0, s, 0)),
            pl.BlockSpec((1, N_PAIR, 3, 16, TS // 16, LANES), lambda b, s: (b, 0, 0, 0, s, 0)),
            pl.BlockSpec((1, TS, D_CONV), lambda b, s: (b, s, 0)),
        ],
        out_shape=[
            jax.ShapeDtypeStruct((B, N_PAIR, 3, S, LANES), bf16),
            jax.ShapeDtypeStruct((B, N_PAIR, 3, 4, S // 4, LANES), bf16),
            jax.ShapeDtypeStruct((B, N_PAIR, 3, 16, S // 16, LANES), bf16),
            jax.ShapeDtypeStruct((B, S, D_CONV), bf16),
        ],
        scratch_shapes=[
            pltpu.VMEM((N_SLAB, TS, LANES), f32),
            pltpu.VMEM((N_SLAB, 4, TS // 4, LANES), f32),
            pltpu.VMEM((8, D_CONV), f32),
            pltpu.VMEM((TS, D), bf16),
            pltpu.VMEM((TS, D_CONV), f32),
            pltpu.VMEM((8 + TS, D_CONV), f32),
        ],
        compiler_params=pltpu.CompilerParams(
            dimension_semantics=("arbitrary", "arbitrary"),
            vmem_limit_bytes=40 * 1024 * 1024),
        name="mixer_in",
    )(x, g, w_in, conv_w, conv_g, gmat)


ATTN_DEPTH = 4
ATTN_EMIT_DEPTH = 3
ATTN_PAIRS = 2


def _attn_scores(q, kcat):
    qt = q.T
    zero = jnp.zeros((HEAD_DIM, BAND), qt.dtype)
    w = jnp.concatenate([jnp.concatenate([qt[:HEAD_DIM], zero], axis=0),
                         jnp.concatenate([zero, qt[HEAD_DIM:]], axis=0)], axis=1)
    return jnp.dot(kcat, w, preferred_element_type=f32)


def _attn_values(st, vt, bias):
    s = st + bias
    m = jnp.max(s, axis=0, keepdims=True)
    p = jnp.exp2(s - m)
    l = jnp.sum(p, axis=0, keepdims=True)
    ot = jnp.dot(vt, p.astype(bf16), preferred_element_type=f32)
    return ot, 1.0 / l, m + jnp.log2(l)


def _attn_rows(ot, rl, lse):
    ot = jnp.concatenate([ot[:HEAD_DIM, :BAND] * rl[:, :BAND],
                          ot[HEAD_DIM:, BAND:] * rl[:, BAND:]], axis=0)
    lset = jnp.concatenate([jnp.broadcast_to(lse[:, :BAND], (HEAD_DIM, BAND)),
                            jnp.broadcast_to(lse[:, BAND:], (HEAD_DIM, BAND))], axis=0)
    return ot.T, lset.T


def _attn_kernel(qkv1_ref, qkv4_ref, qkv16_ref,
                 bias_ref, g_ref, gmat_ref, out_ref, o_s, lse_s, vt_s, st_s, ot_s, stat_s):
    S = qkv1_ref.shape[3]
    first_pair = pl.program_id(1) * ATTN_PAIRS

    blocks = []
    for pp in range(ATTN_PAIRS):
        per_branch = []
        for bi, (d, ref) in enumerate(zip(DILATIONS, (qkv1_ref, qkv4_ref, qkv16_ref))):
            per_branch.append([(bi, d, ref, () if d == 1 else (r,), r, n, pp)
                               for n in range(S // d // BAND) for r in range(d)])
        blocks += [blk for two in zip(per_branch[2], per_branch[1]) for blk in two] + per_branch[0]
    per_pair = len(blocks) // ATTN_PAIRS
    last_strided = per_pair - S // BAND - 1

    nblk = {d: S // d // BAND for d in DILATIONS}

    def flat(blk):
        bi, d, _, _, r, n, _ = blk
        return bi * (S // BAND) + r * nblk[d] + n

    def start(blk, t):
        _, _, ref, cls, _, n, pp = blk
        rows = pl.ds(n * BAND, BAND)
        vt_s[flat(blk)] = ref[(0, pp, 2) + cls + (rows,)].T
        keys = pl.ds((n - 1) * BAND, 2 * BAND) if n else rows
        st = _attn_scores(ref[(0, pp, 0) + cls + (rows,)], ref[(0, pp, 1) + cls + (keys,)])
        st_s[t % st_s.shape[0], :st.shape[0]] = st

    def finish(blk, t):
        bi, _, _, _, _, n, pp = blk
        slot = t % st_s.shape[0]
        if n:
            vt = jnp.concatenate([vt_s[flat(blk) - 1], vt_s[flat(blk)]], axis=1)
            ot, rl, lse = _attn_values(st_s[slot], vt, bias_ref[bi, first_pair + pp])
        else:
            ot, rl, lse = _attn_values(st_s[slot, :BAND], vt_s[flat(blk)],
                                       bias_ref[bi, first_pair + pp, BAND:, :])
        slot = t % ot_s.shape[0]
        ot_s[slot] = ot
        stat_s[slot, 0:1] = rl
        stat_s[slot, 1:2] = lse

    def emit(blk, t):
        bi, d, _, _, r, n, pp = blk
        slot = t % ot_s.shape[0]
        o, lse = _attn_rows(ot_s[slot], stat_s[slot, 0:1], stat_s[slot, 1:2])
        if d == 16:
            rows, dst = pl.ds((r % 4) * (S // 4) + r // 4, BAND, stride=4), 3
        else:
            rows, dst = (pl.ds(n * (BAND * d) + r, BAND, stride=d) if d > 1
                         else pl.ds(n * BAND, BAND)), bi
        o_s[pp, dst, rows, :] = o
        lse_s[pp, dst, rows, :] = lse

    def untangle(pp, r4, m):
        src = pl.ds(r4 * (S // 4) + m * BAND, BAND)
        dst = pl.ds(m * (4 * BAND) + r4, BAND, stride=4)
        o_s[pp, 2, dst, :] = o_s[pp, 3, src, :]
        lse_s[pp, 2, dst, :] = lse_s[pp, 3, src, :]

    def merge(pp, j):
        rows = pl.ds(j * BAND, BAND)
        l1, l4, l16 = lse_s[pp, 0, rows, :], lse_s[pp, 1, rows, :], lse_s[pp, 2, rows, :]
        lm = jnp.maximum(jnp.maximum(l1, l4), l16)
        w1, w4, w16 = jnp.exp2(l1 - lm), jnp.exp2(l4 - lm), jnp.exp2(l16 - lm)
        num = o_s[pp, 0, rows, :] * w1 + o_s[pp, 1, rows, :] * w4 + o_s[pp, 2, rows, :] * w16
        a = num / (w1 + w4 + w16)
        ss = jnp.dot((a * a).astype(bf16), gmat_ref[...], preferred_element_type=f32)
        an = a * lax.rsqrt(ss * (1.0 / HEAD_DIM) + EPS) * g_ref[pp]
        out_ref[0, pp, rows, :] = an.astype(bf16)

    for t in range(len(blocks) + ATTN_DEPTH + ATTN_EMIT_DEPTH):
        if t < len(blocks):
            start(blocks[t], t)
        if 0 <= t - ATTN_DEPTH < len(blocks):
            finish(blocks[t - ATTN_DEPTH], t - ATTN_DEPTH)
        done = t - ATTN_DEPTH - ATTN_EMIT_DEPTH
        if 0 <= done:
            blk = blocks[done]
            emit(blk, done)
            if done % per_pair == last_strided:
                for r4 in range(4):
                    for m in range(S // 4 // BAND):
                        untangle(blk[6], r4, m)
            if blk[0] == 0:
                merge(blk[6], blk[5])


def _attention(qkv1, qkv4, qkv16, bias, attn_g, gmat):
    B, _, _, S, _ = qkv1.shape
    pp = ATTN_PAIRS
    return pl.pallas_call(
        _attn_kernel,
        grid=(B, N_PAIR // pp),
        in_specs=[pl.BlockSpec((1, pp, 3, S, LANES), lambda b, p: (b, p, 0, 0, 0)),
                  pl.BlockSpec((1, pp, 3, 4, S // 4, LANES), lambda b, p: (b, p, 0, 0, 0, 0)),
                  pl.BlockSpec((1, pp, 3, 16, S // 16, LANES), lambda b, p: (b, p, 0, 0, 0, 0)),
                  pl.BlockSpec(bias.shape, lambda b, p: (0, 0, 0, 0), pipeline_mode=pl.Buffered(1)),
                  pl.BlockSpec((pp, 1, LANES), lambda b, p: (p, 0, 0)),
                  pl.BlockSpec(gmat.shape, lambda b, p: (0, 0), pipeline_mode=pl.Buffered(1))],
        out_specs=pl.BlockSpec((1, pp, S, LANES), lambda b, p: (b, p, 0, 0)),
        out_shape=jax.ShapeDtypeStruct((B, N_PAIR, S, LANES), bf16),
        scratch_shapes=[pltpu.VMEM((pp, 4, S, LANES), f32), pltpu.VMEM((pp, 4, S, LANES), f32),
                        pltpu.VMEM((3 * S // BAND, LANES, BAND), bf16),
                        pltpu.VMEM((ATTN_DEPTH + 1, 2 * BAND, 2 * BAND), f32),
                        pltpu.VMEM((ATTN_EMIT_DEPTH + 1, LANES, 2 * BAND), f32),
                        pltpu.VMEM((ATTN_EMIT_DEPTH + 1, 8, 2 * BAND), f32)],
        compiler_params=pltpu.CompilerParams(
            dimension_semantics=("arbitrary", "arbitrary"),
            vmem_limit_bytes=52 * 1024 * 1024),
        name="dilated_attention",
    )(qkv1, qkv4, qkv16, bias, attn_g, gmat)


def _out_ffn_kernel(a_ref, y_ref, x_ref, wo_ref, g2_ref, up_ref, cw_ref, down_ref, gf_ref,
                    out_ref, h2_s, acc_s, carry_s, *rings, final):
    up_slots, act_slots = rings[:FFN_DEPTH], rings[FFN_DEPTH:]
    si = pl.program_id(1)
    mix = jnp.concatenate([a_ref[0, p] for p in range(N_PAIR)] + [y_ref[0]], axis=1)
    x1 = x_ref[0] + jnp.dot(mix, wo_ref[...], preferred_element_type=f32)
    h2_s[...] = (x1 * _rms_scale(x1) * g2_ref[...]).astype(bf16)
    acc_s[...] = x1

    @pl.when(si == 0)
    def _():
        carry_s[...] = jnp.zeros_like(carry_s)

    def halves(c):
        return [pl.ds(half * D_FF + c * FF_CHUNK, FF_CHUNK) for half in range(2)]

    def up_proj(c):
        up_s = up_slots[c % FFN_DEPTH]
        up_s[:8] = carry_s[c]
        for half, cols in enumerate(halves(c)):
            up_s[8:, half * FF_CHUNK:(half + 1) * FF_CHUNK] = jnp.dot(
                h2_s[...], up_ref[:, cols], preferred_element_type=f32)
        carry_s[c] = up_s[TS_FFN:]

    def activate(c):
        up_s = up_slots[c % FFN_DEPTH]
        cw = jnp.concatenate([cw_ref[:, cols] for cols in halves(c)], axis=1)
        conv = cw[0:1] * up_s[6:6 + TS_FFN] + cw[1:2] * up_s[7:7 + TS_FFN] + cw[2:3] * up_s[8:]
        gate, val = conv[:, :FF_CHUNK], conv[:, FF_CHUNK:]
        act = (gate / (1.0 + jnp.exp(-gate)) * val).astype(bf16)
        g = FFN_DOWN_GROUP
        act_slots[c // g % 2][:, c % g * FF_CHUNK:(c % g + 1) * FF_CHUNK] = act

    def down_proj(c):
        first = c - c % FFN_DOWN_GROUP
        width = (c - first + 1) * FF_CHUNK
        acc_s[...] += jnp.dot(act_slots[c // FFN_DOWN_GROUP % 2][:, :width],
                              down_ref[pl.ds(first * FF_CHUNK, width), :],
                              preferred_element_type=f32)

    for t in range(N_FF_CHUNK + FFN_DEPTH):
        if t < N_FF_CHUNK:
            up_proj(t)
        if 0 <= t - (FFN_DEPTH - 1) < N_FF_CHUNK:
            activate(t - (FFN_DEPTH - 1))
        c = t - FFN_DEPTH
        if 0 <= c and ((c + 1) % FFN_DOWN_GROUP == 0 or c == N_FF_CHUNK - 1):
            down_proj(c)

    out = acc_s[...]
    if final:
        out = out * _rms_scale(out) * gf_ref[...]
    out_ref[0] = out


def _out_ffn(attn, y, x, w_out, g2, up_w, conv_w, down_w, gf, layer, final):
    B, S, D = x.shape
    ns = S // TS_FFN
    const = dict(pipeline_mode=pl.Buffered(1))
    return pl.pallas_call(
        functools.partial(_out_ffn_kernel, final=final),
        grid=(B, ns),
        in_specs=[
            pl.BlockSpec((1, N_PAIR, TS_FFN, LANES), lambda b, s: (b, 0, s, 0)),
            pl.BlockSpec((1, TS_FFN, D_CONV), lambda b, s: (b, s, 0)),
            pl.BlockSpec((1, TS_FFN, D), lambda b, s: (b, s, 0)),
            pl.BlockSpec((None,) + w_out.shape[1:], lambda b, s: (layer, 0, 0), **const),
            pl.BlockSpec((1, D), lambda b, s: (0, 0), **const),
            pl.BlockSpec((None,) + up_w.shape[1:], lambda b, s: (layer, 0, 0), **const),
            pl.BlockSpec(conv_w.shape, lambda b, s: (0, 0), **const),
            pl.BlockSpec((None,) + down_w.shape[1:], lambda b, s: (layer, 0, 0), **const),
            pl.BlockSpec((1, D), lambda b, s: (0, 0), **const),
        ],
        out_specs=pl.BlockSpec((1, TS_FFN, D), lambda b, s: (b, s, 0)),
        out_shape=jax.ShapeDtypeStruct((B, S, D), f32),
        scratch_shapes=[
            pltpu.VMEM((TS_FFN, D), bf16),
            pltpu.VMEM((TS_FFN, D), f32),
            pltpu.VMEM((N_FF_CHUNK, 8, 2 * FF_CHUNK), f32),
        ] + [pltpu.VMEM((8 + TS_FFN, 2 * FF_CHUNK), f32)] * FFN_DEPTH + [
            pltpu.VMEM((TS_FFN, FFN_DOWN_GROUP * FF_CHUNK), bf16)] * 2 + [
        ],
        compiler_params=pltpu.CompilerParams(
            dimension_semantics=("arbitrary", "arbitrary"),
            vmem_limit_bytes=48 * 1024 * 1024),
        name="out_ffn_final" if final else "out_ffn",
    )(attn, y, x, w_out, g2, up_w, conv_w, down_w, gf)


def _alibi_bias():
    slopes = 2.0 ** (-8.0 * jnp.arange(1, N_HEADS + 1, dtype=f32) / N_HEADS)
    i = jnp.arange(BAND)[None, :]
    j = jnp.arange(2 * BAND)[:, None]
    dist = BAND + i - j
    valid = (dist >= 0) & (dist <= BAND)
    per_d = []
    for d in DILATIONS:
        b = -(slopes * (d * LOG2E))[:, None, None] * dist.astype(f32)[None]
        b = jnp.where(valid[None], b, NEG).reshape(N_PAIR, 2, 2 * BAND, BAND)
        per_d.append(b.transpose(0, 2, 1, 3).reshape(N_PAIR, 2 * BAND, 2 * BAND))
    return jnp.stack(per_d)


def _group_ones(n):
    g = jnp.arange(n) // HEAD_DIM
    return (g[:, None] == g[None, :]).astype(bf16)


def kernel(x, norm1_g, w_in, mix_conv_w, attn_out_g, conv_out_g, w_out, norm2_g, ffn_up,
           ffn_conv_w, ffn_down, final_norm_g):
    depth = w_in.shape[0]
    bias = _alibi_bias()
    gmat_c = _group_ones(D_CONV)
    gmat_a = _group_ones(LANES)
    gf = final_norm_g.reshape(1, D_MODEL)
    w_in, w_out, ffn_up, ffn_down = (w.astype(bf16) for w in (w_in, w_out, ffn_up, ffn_down))
    for layer in range(depth):
        qkv1, qkv4, qkv16, y = _mixer_in(
            x, norm1_g[layer].reshape(1, D_MODEL), w_in, mix_conv_w[layer],
            conv_out_g[layer].reshape(1, D_CONV), gmat_c, layer)
        attn = _attention(qkv1, qkv4, qkv16, bias,
                          attn_out_g[layer].reshape(N_PAIR, 1, LANES), gmat_a)
        x = _out_ffn(attn, y, x, w_out, norm2_g[layer].reshape(1, D_MODEL),
                     ffn_up, ffn_conv_w[layer], ffn_down, gf, layer, final=(layer == depth - 1))
    return x
```

```python
import functools
import math

import jax
import jax.numpy as jnp
from jax import lax
from jax.experimental import pallas as pl
from jax.experimental.pallas import tpu as pltpu

D_MODEL = 1024
D_ATTN = 512
D_CONV = 512
HEAD_DIM = 64
N_HEADS = D_ATTN // HEAD_DIM
DILATIONS = (1, 4, 16)
BAND = 128
D_FF = 2816
EPS = 1e-6

LANES = 128
N_SLAB = 3 * D_ATTN // LANES
N_PAIR = N_HEADS // 2
FF_CHUNK = 256
N_FF_CHUNK = D_FF // FF_CHUNK
FFN_DEPTH = 3
FFN_DOWN_GROUP = 2
TS = 512
TS_FFN = 512
LOG2E = 1.0 / math.log(2.0)
Q_SCALE = HEAD_DIM ** -0.5 * LOG2E
NEG = -1e30

f32 = jnp.float32
bf16 = jnp.bfloat16


def _rms_scale(x):
    return lax.rsqrt(jnp.mean(x * x, axis=-1, keepdims=True) + EPS)


def _mixer_in_kernel(x_ref, g_ref, w_ref, cw_ref, cg_ref, gmat_ref,
                     o1_ref, o4_ref, o16_ref, y_ref, nat_s, d4_s, carry_s, h_s, gb_s, cu_s):
    si = pl.program_id(1)

    @pl.when(si == 0)
    def _():
        carry_s[...] = jnp.zeros_like(carry_s)

    x = x_ref[0]
    h_s[...] = (x * _rms_scale(x) * g_ref[...]).astype(bf16)

    def project(col0):
        return jnp.dot(h_s[...], w_ref[:, col0:col0 + D_ATTN], preferred_element_type=f32)

    c0 = 3 * D_ATTN
    gb_s[...] = project(c0)
    cu_s[:8] = carry_s[...]
    cu_s[8:] = project(c0 + D_CONV) * project(c0 + 2 * D_CONV)
    carry_s[...] = cu_s[TS:]

    def conv_branch():
        cw = cw_ref[...]
        y = gb_s[...] * (cw[0:1] * cu_s[6:6 + TS] + cw[1:2] * cu_s[7:7 + TS] + cw[2:3] * cu_s[8:])
        ss = jnp.dot((y * y).astype(bf16), gmat_ref[...], preferred_element_type=f32)
        y_ref[0] = (y * lax.rsqrt(ss * (1.0 / HEAD_DIM) + EPS) * cg_ref[...]).astype(bf16)

    def qkv_half(half):
        proj = jnp.dot(h_s[...], w_ref[:, half * 2 * LANES:(half + 1) * 2 * LANES],
                       preferred_element_type=f32)
        if half < N_PAIR // 2:
            proj = proj * Q_SCALE
        for p in range(2):
            j = 2 * half + p
            pj = proj[:, p * LANES:(p + 1) * LANES]
            nat_s[j] = pj
            o1_ref[0, j % N_PAIR, j // N_PAIR] = pj.astype(bf16)

    def relayout(half):
        for j in (2 * half, 2 * half + 1):
            for r in range(4):
                c4 = nat_s[j, pl.ds(r, TS // 4, stride=4), :]
                d4_s[j, r] = c4
                o4_ref[0, j % N_PAIR, j // N_PAIR, 0, r] = c4.astype(bf16)
            for r in range(16):
                c16 = d4_s[j, r % 4, pl.ds(r // 4, TS // 16, stride=4), :]
                o16_ref[0, j % N_PAIR, j // N_PAIR, 0, r] = c16.astype(bf16)

    n_half = N_SLAB // 2
    qkv_half(0)
    conv_branch()
    for half in range(1, n_half):
        qkv_half(half)
        relayout(half - 1)
    relayout(n_half - 1)


def _mixer_in(x, g, w_in, conv_w, conv_g, gmat, layer):
    B, S, D = x.shape
    ns = S // TS
    const = dict(pipeline_mode=pl.Buffered(1))
    return pl.pallas_call(
        _mixer_in_kernel,
        grid=(B, ns),
        in_specs=[
            pl.BlockSpec((1, TS, D), lambda b, s: (b, s, 0)),
            pl.BlockSpec((1, D), lambda b, s: (0, 0), **const),
            pl.BlockSpec((None,) + w_in.shape[1:], lambda b, s: (layer, 0, 0), **const),
            pl.BlockSpec(conv_w.shape, lambda b, s: (0, 0), **const),
            pl.BlockSpec((1, D_CONV), lambda b, s: (0, 0), **const),
            pl.BlockSpec(gmat.shape, lambda b, s: (0, 0), **const),
        ],
        out_specs=[
            pl.BlockSpec((1, N_PAIR, 3, TS, LANES), lambda b, s: (b, 0, 0, s, 0)),
            pl.BlockSpec((1, N_PAIR, 3, 1, 4, TS // 4, LANES), lambda b, s: (b, 0, 0, s, 0, 0, 0)),
            pl.BlockSpec((1, N_PAIR, 3, 1, 16, TS // 16, LANES), lambda b, s: (b, 0, 0, s, 0, 0, 0)),
            pl.BlockSpec((1, TS, D_CONV), lambda b, s: (b, s, 0)),
        ],
        out_shape=[
            jax.ShapeDtypeStruct((B, N_PAIR, 3, S, LANES), bf16),
            jax.ShapeDtypeStruct((B, N_PAIR, 3, ns, 4, TS // 4, LANES), bf16),
            jax.ShapeDtypeStruct((B, N_PAIR, 3, ns, 16, TS // 16, LANES), bf16),
            jax.ShapeDtypeStruct((B, S, D_CONV), bf16),
        ],
        scratch_shapes=[
            pltpu.VMEM((N_SLAB, TS, LANES), f32),
            pltpu.VMEM((N_SLAB, 4, TS // 4, LANES), f32),
            pltpu.VMEM((8, D_CONV), f32),
            pltpu.VMEM((TS, D), bf16),
            pltpu.VMEM((TS, D_CONV), f32),
            pltpu.VMEM((8 + TS, D_CONV), f32),
        ],
        compiler_params=pltpu.CompilerParams(
            dimension_semantics=("arbitrary", "arbitrary"),
            vmem_limit_bytes=40 * 1024 * 1024),
        name="mixer_in",
    )(x, g, w_in, conv_w, conv_g, gmat)


ATTN_DEPTH = 4
ATTN_EMIT_DEPTH = 3


def _attn_scores(q, kcat):
    qt = q.T
    zero = jnp.zeros((HEAD_DIM, BAND), qt.dtype)
    w = jnp.concatenate([jnp.concatenate([qt[:HEAD_DIM], zero], axis=0),
                         jnp.concatenate([zero, qt[HEAD_DIM:]], axis=0)], axis=1)
    return jnp.dot(kcat, w, preferred_element_type=f32)


def _attn_values(st, vt, bias):
    s = st + bias
    m = jnp.max(s, axis=0, keepdims=True)
    p = jnp.exp2(s - m)
    l = jnp.sum(p, axis=0, keepdims=True)
    ot = jnp.dot(vt, p.astype(bf16), preferred_element_type=f32)
    return ot, 1.0 / l, m + jnp.log2(l)


def _attn_rows(ot, rl, lse):
    ot = jnp.concatenate([ot[:HEAD_DIM, :BAND] * rl[:, :BAND],
                          ot[HEAD_DIM:, BAND:] * rl[:, BAND:]], axis=0)
    lset = jnp.concatenate([jnp.broadcast_to(lse[:, :BAND], (HEAD_DIM, BAND)),
                            jnp.broadcast_to(lse[:, BAND:], (HEAD_DIM, BAND))], axis=0)
    return ot.T, lset.T


def _attn_kernel(qkv1_ref, qkv4_ref, qkv16_ref,
                 bias_ref, g_ref, gmat_ref, out_ref, o_s, lse_s, vt_s, st_s, ot_s, stat_s):
    S = qkv1_ref.shape[3]
    pair = pl.program_id(1)

    per_branch = []
    for bi, (d, ref) in enumerate(zip(DILATIONS, (qkv1_ref, qkv4_ref, qkv16_ref))):
        per_branch.append([(bi, d, ref, None, r, n)
                           for n in range(S // d // BAND) for r in range(d)])

    def rows_of(blk, part, n):
        _, d, ref, _, r, _ = blk
        if d == 1:
            return ref[0, 0, part, pl.ds(n * BAND, BAND)]
        pieces = BAND // ref.shape[5]
        tiles = [ref[0, 0, part, n * pieces + i, r] for i in range(pieces)]
        return tiles[0] if pieces == 1 else jnp.concatenate(tiles, axis=0)
    blocks = [blk for pair in zip(per_branch[2], per_branch[1]) for blk in pair] + per_branch[0]

    nblk = {d: S // d // BAND for d in DILATIONS}

    def flat(blk):
        bi, d, _, _, r, n = blk
        return bi * (S // BAND) + r * nblk[d] + n

    def start(blk, t):
        n = blk[5]
        vt_s[flat(blk)] = rows_of(blk, 2, n).T
        keys = rows_of(blk, 1, n)
        if n:
            keys = jnp.concatenate([rows_of(blk, 1, n - 1), keys], axis=0)
        st = _attn_scores(rows_of(blk, 0, n), keys)
        st_s[t % st_s.shape[0], :st.shape[0]] = st

    def finish(blk, t):
        bi, _, _, _, _, n = blk
        slot = t % st_s.shape[0]
        if n:
            vt = jnp.concatenate([vt_s[flat(blk) - 1], vt_s[flat(blk)]], axis=1)
            ot, rl, lse = _attn_values(st_s[slot], vt, bias_ref[bi, pair])
        else:
            ot, rl, lse = _attn_values(st_s[slot, :BAND], vt_s[flat(blk)],
                                       bias_ref[bi, pair, BAND:, :])
        slot = t % ot_s.shape[0]
        ot_s[slot] = ot
        stat_s[slot, 0:1] = rl
        stat_s[slot, 1:2] = lse

    def emit(blk, t):
        bi, d, _, _, r, n = blk
        slot = t % ot_s.shape[0]
        o, lse = _attn_rows(ot_s[slot], stat_s[slot, 0:1], stat_s[slot, 1:2])
        if d == 16:
            rows, dst = pl.ds((r % 4) * (S // 4) + r // 4, BAND, stride=4), 3
        else:
            rows, dst = (pl.ds(n * (BAND * d) + r, BAND, stride=d) if d > 1
                         else pl.ds(n * BAND, BAND)), bi
        o_s[dst, rows, :] = o
        lse_s[dst, rows, :] = lse

    def untangle(r4, m):
        src = pl.ds(r4 * (S // 4) + m * BAND, BAND)
        dst = pl.ds(m * (4 * BAND) + r4, BAND, stride=4)
        o_s[2, dst, :] = o_s[3, src, :]
        lse_s[2, dst, :] = lse_s[3, src, :]

    def merge(j):
        rows = pl.ds(j * BAND, BAND)
        l1, l4, l16 = lse_s[0, rows, :], lse_s[1, rows, :], lse_s[2, rows, :]
        lm = jnp.maximum(jnp.maximum(l1, l4), l16)
        w1, w4, w16 = jnp.exp2(l1 - lm), jnp.exp2(l4 - lm), jnp.exp2(l16 - lm)
        num = o_s[0, rows, :] * w1 + o_s[1, rows, :] * w4 + o_s[2, rows, :] * w16
        a = num / (w1 + w4 + w16)
        ss = jnp.dot((a * a).astype(bf16), gmat_ref[...], preferred_element_type=f32)
        an = a * lax.rsqrt(ss * (1.0 / HEAD_DIM) + EPS) * g_ref[0]
        out_ref[0, 0, rows, :] = an.astype(bf16)

    for t in range(len(blocks) + ATTN_DEPTH + ATTN_EMIT_DEPTH):
        if t < len(blocks):
            start(blocks[t], t)
        if 0 <= t - ATTN_DEPTH < len(blocks):
            finish(blocks[t - ATTN_DEPTH], t - ATTN_DEPTH)
        done = t - ATTN_DEPTH - ATTN_EMIT_DEPTH
        if 0 <= done:
            emit(blocks[done], done)
            if done == len(per_branch[2]) + len(per_branch[1]) - 1:
                for r4 in range(4):
                    for m in range(S // 4 // BAND):
                        untangle(r4, m)
            if blocks[done][0] == 0:
                merge(blocks[done][5])


def _attention(qkv1, qkv4, qkv16, bias, attn_g, gmat):
    B, _, _, S, _ = qkv1.shape
    return pl.pallas_call(
        _attn_kernel,
        grid=(B, N_PAIR),
        in_specs=[pl.BlockSpec((1, 1, 3, S, LANES), lambda b, p: (b, p, 0, 0, 0)),
                  pl.BlockSpec((1, 1) + qkv4.shape[2:], lambda b, p: (b, p, 0, 0, 0, 0, 0)),
                  pl.BlockSpec((1, 1) + qkv16.shape[2:], lambda b, p: (b, p, 0, 0, 0, 0, 0)),
                  pl.BlockSpec(bias.shape, lambda b, p: (0, 0, 0, 0), pipeline_mode=pl.Buffered(1)),
                  pl.BlockSpec((1, 1, LANES), lambda b, p: (p, 0, 0)),
                  pl.BlockSpec(gmat.shape, lambda b, p: (0, 0), pipeline_mode=pl.Buffered(1))],
        out_specs=pl.BlockSpec((1, 1, S, LANES), lambda b, p: (b, p, 0, 0)),
        out_shape=jax.ShapeDtypeStruct((B, N_PAIR, S, LANES), bf16),
        scratch_shapes=[pltpu.VMEM((4, S, LANES), f32), pltpu.VMEM((4, S, LANES), f32),
                        pltpu.VMEM((3 * S // BAND, LANES, BAND), bf16),
                        pltpu.VMEM((ATTN_DEPTH + 1, 2 * BAND, 2 * BAND), f32),
                        pltpu.VMEM((ATTN_EMIT_DEPTH + 1, LANES, 2 * BAND), f32),
                        pltpu.VMEM((ATTN_EMIT_DEPTH + 1, 8, 2 * BAND), f32)],
        compiler_params=pltpu.CompilerParams(
            dimension_semantics=("arbitrary", "arbitrary"),
            vmem_limit_bytes=40 * 1024 * 1024),
        name="dilated_attention",
    )(qkv1, qkv4, qkv16, bias, attn_g, gmat)


def _out_ffn_kernel(a_ref, y_ref, x_ref, wo_ref, g2_ref, up_ref, cw_ref, down_ref, gf_ref,
                    out_ref, h2_s, acc_s, carry_s, *rings, final):
    up_slots, act_slots = rings[:FFN_DEPTH], rings[FFN_DEPTH:]
    si = pl.program_id(1)
    mix = jnp.concatenate([a_ref[0, p] for p in range(N_PAIR)] + [y_ref[0]], axis=1)
    x1 = x_ref[0] + jnp.dot(mix, wo_ref[...], preferred_element_type=f32)
    h2_s[...] = (x1 * _rms_scale(x1) * g2_ref[...]).astype(bf16)
    acc_s[...] = x1

    @pl.when(si == 0)
    def _():
        carry_s[...] = jnp.zeros_like(carry_s)

    def halves(c):
        return [pl.ds(half * D_FF + c * FF_CHUNK, FF_CHUNK) for half in range(2)]

    def up_proj(c):
        up_s = up_slots[c % FFN_DEPTH]
        up_s[:8] = carry_s[c]
        for half, cols in enumerate(halves(c)):
            up_s[8:, half * FF_CHUNK:(half + 1) * FF_CHUNK] = jnp.dot(
                h2_s[...], up_ref[:, cols], preferred_element_type=f32)
        carry_s[c] = up_s[TS_FFN:]

    def activate(c):
        up_s = up_slots[c % FFN_DEPTH]
        cw = jnp.concatenate([cw_ref[:, cols] for cols in halves(c)], axis=1)
        conv = cw[0:1] * up_s[6:6 + TS_FFN] + cw[1:2] * up_s[7:7 + TS_FFN] + cw[2:3] * up_s[8:]
        gate, val = conv[:, :FF_CHUNK], conv[:, FF_CHUNK:]
        act = (gate / (1.0 + jnp.exp(-gate)) * val).astype(bf16)
        group, first = down_group(c)
        act_slots[group % 2][:, (c - first) * FF_CHUNK:(c - first + 1) * FF_CHUNK] = act

    def down_group(c):
        shift = -N_FF_CHUNK % FFN_DOWN_GROUP
        group = (c + shift) // FFN_DOWN_GROUP
        return group, max(group * FFN_DOWN_GROUP - shift, 0)

    def down_proj(c):
        group, first = down_group(c)
        width = (c - first + 1) * FF_CHUNK
        acc_s[...] += jnp.dot(act_slots[group % 2][:, :width],
                              down_ref[pl.ds(first * FF_CHUNK, width), :],
                              preferred_element_type=f32)

    for t in range(N_FF_CHUNK + FFN_DEPTH):
        if t < N_FF_CHUNK:
            up_proj(t)
        if 0 <= t - (FFN_DEPTH - 1) < N_FF_CHUNK:
            activate(t - (FFN_DEPTH - 1))
        c = t - FFN_DEPTH
        if 0 <= c and down_group(c)[0] != down_group(min(c + 1, N_FF_CHUNK))[0]:
            down_proj(c)

    out = acc_s[...]
    if final:
        out = out * _rms_scale(out) * gf_ref[...]
    out_ref[0] = out


def _out_ffn(attn, y, x, w_out, g2, up_w, conv_w, down_w, gf, layer, final):
    B, S, D = x.shape
    ns = S // TS_FFN
    const = dict(pipeline_mode=pl.Buffered(1))
    return pl.pallas_call(
        functools.partial(_out_ffn_kernel, final=final),
        grid=(B, ns),
        in_specs=[
            pl.BlockSpec((1, N_PAIR, TS_FFN, LANES), lambda b, s: (b, 0, s, 0)),
            pl.BlockSpec((1, TS_FFN, D_CONV), lambda b, s: (b, s, 0)),
            pl.BlockSpec((1, TS_FFN, D), lambda b, s: (b, s, 0)),
            pl.BlockSpec((None,) + w_out.shape[1:], lambda b, s: (layer, 0, 0), **const),
            pl.BlockSpec((1, D), lambda b, s: (0, 0), **const),
            pl.BlockSpec((None,) + up_w.shape[1:], lambda b, s: (layer, 0, 0), **const),
            pl.BlockSpec(conv_w.shape, lambda b, s: (0, 0), **const),
            pl.BlockSpec((None,) + down_w.shape[1:], lambda b, s: (layer, 0, 0), **const),
            pl.BlockSpec((1, D), lambda b, s: (0, 0), **const),
        ],
        out_specs=pl.BlockSpec((1, TS_FFN, D), lambda b, s: (b, s, 0)),
        out_shape=jax.ShapeDtypeStruct((B, S, D), f32),
        scratch_shapes=[
            pltpu.VMEM((TS_FFN, D), bf16),
            pltpu.VMEM((TS_FFN, D), f32),
            pltpu.VMEM((N_FF_CHUNK, 8, 2 * FF_CHUNK), f32),
        ] + [pltpu.VMEM((8 + TS_FFN, 2 * FF_CHUNK), f32)] * FFN_DEPTH + [
            pltpu.VMEM((TS_FFN, FFN_DOWN_GROUP * FF_CHUNK), bf16)] * 2 + [
        ],
        compiler_params=pltpu.CompilerParams(
            dimension_semantics=("arbitrary", "arbitrary"),
            vmem_limit_bytes=48 * 1024 * 1024),
        name="out_ffn_final" if final else "out_ffn",
    )(attn, y, x, w_out, g2, up_w, conv_w, down_w, gf)


def _alibi_bias():
    slopes = 2.0 ** (-8.0 * jnp.arange(1, N_HEADS + 1, dtype=f32) / N_HEADS)
    i = jnp.arange(BAND)[None, :]
    j = jnp.arange(2 * BAND)[:, None]
    dist = BAND + i - j
    valid = (dist >= 0) & (dist <= BAND)
    per_d = []
    for d in DILATIONS:
        b = -(slopes * (d * LOG2E))[:, None, None] * dist.astype(f32)[None]
        b = jnp.where(valid[None], b, NEG).reshape(N_PAIR, 2, 2 * BAND, BAND)
        per_d.append(b.transpose(0, 2, 1, 3).reshape(N_PAIR, 2 * BAND, 2 * BAND))
    return jnp.stack(per_d)


def _group_ones(n):
    g = jnp.arange(n) // HEAD_DIM
    return (g[:, None] == g[None, :]).astype(bf16)


def kernel(x, norm1_g, w_in, mix_conv_w, attn_out_g, conv_out_g, w_out, norm2_g, ffn_up,
           ffn_conv_w, ffn_down, final_norm_g):
    depth = w_in.shape[0]
    bias = _alibi_bias()
    gmat_c = _group_ones(D_CONV)
    gmat_a = _group_ones(LANES)
    gf = final_norm_g.reshape(1, D_MODEL)
    w_in, w_out, ffn_up, ffn_down = (w.astype(bf16) for w in (w_in, w_out, ffn_up, ffn_down))
    for layer in range(depth):
        qkv1, qkv4, qkv16, y = _mixer_in(
            x, norm1_g[layer].reshape(1, D_MODEL), w_in, mix_conv_w[layer],
            conv_out_g[layer].reshape(1, D_CONV), gmat_c, layer)
        attn = _attention(qkv1, qkv4, qkv16, bias,
                          attn_out_g[layer].reshape(N_PAIR, 1, LANES), gmat_a)
        x = _out_ffn(attn, y, x, w_out, norm2_g[layer].reshape(1, D_MODEL),
                     ffn_up, ffn_conv_w[layer], ffn_down, gf, layer, final=(layer == depth - 1))
    return x
```
